```python
import math
import jax, jax.numpy as jnp
from jax import lax
import numpy as np

D_MODEL = 1024
BATCH = 8
SEQ = 8192
DEPTH = 2

D_MIX = D_MODEL
S5_WIDTH = D_MIX // 4
S5_GROUP = 16
S5_GROUPS = S5_WIDTH // S5_GROUP
S5_STATE = 64
S5_MIN_DECAY = 1e-4
GLA_WIDTH = D_MIX // 2
GLA_HEADS = 4
GLA_DV = GLA_WIDTH // GLA_HEADS
GLA_DK = GLA_DV // 2
GLA_KEY_WIDTH = GLA_HEADS * GLA_DK
GLA_GATE_RANK = 16
GLA_TAU = 16.0
GLA_CHUNK = 64
LRU_WIDTH = D_MIX - S5_WIDTH - GLA_WIDTH
LRU_BLOCKS = 8
LRU_BLOCK = LRU_WIDTH // LRU_BLOCKS
LRU_CONV = 4
LRU_C = 8.0
D_FF = 4 * D_MODEL
DEEPNORM_ALPHA = (2 * DEPTH) ** 0.25
DEEPNORM_BETA = (8 * DEPTH) ** -0.25
LN_EPS = 1e-5
RMS_EPS = 1e-6
SPLIT_SIZES = (S5_WIDTH, GLA_KEY_WIDTH, GLA_KEY_WIDTH, GLA_WIDTH, GLA_WIDTH,
               GLA_GATE_RANK, LRU_WIDTH, LRU_WIDTH)
D_IN = sum(SPLIT_SIZES)

kernel_name = 'hymba_style_s5_gla_rglru_deepnorm'


def _layer_norm(x, g, b):
    xf = x.astype(jnp.float32)
    mu = jnp.mean(xf, axis=-1, keepdims=True)
    var = jnp.mean(jnp.square(xf - mu), axis=-1, keepdims=True)
    return ((xf - mu) * lax.rsqrt(var + LN_EPS) * g + b).astype(x.dtype)


def _complex_affine_combine(e1, e2):
    ar1, ai1, br1, bi1 = e1
    ar2, ai2, br2, bi2 = e2
    ar = ar2 * ar1 - ai2 * ai1
    ai = ar2 * ai1 + ai2 * ar1
    br = ar2 * br1 - ai2 * bi1 + br2
    bi = ar2 * bi1 + ai2 * br1 + bi2
    return (ar, ai, br, bi)


def _real_affine_combine(e1, e2):
    a1, b1 = e1
    a2, b2 = e2
    return (a2 * a1, a2 * b1 + b2)


def _s5_mixer(u, lam_re, lam_im, log_dt, b_re, b_im, c_re, c_im, d_skip, w_glu):
    bsz, seq, _ = u.shape
    f32 = jnp.float32
    u = u.astype(f32)
    ug = u.reshape(bsz, seq, S5_GROUPS, S5_GROUP)
    dt = jnp.exp(log_dt.astype(f32))[:, None]
    lr = jnp.minimum(lam_re.astype(f32), -S5_MIN_DECAY)
    li = lam_im.astype(f32)
    mag = jnp.exp(lr * dt)
    a_re = mag * jnp.cos(li * dt)
    a_im = mag * jnp.sin(li * dt)
    den = lr * lr + li * li
    f_re = ((a_re - 1.0) * lr + a_im * li) / den
    f_im = (a_im * lr - (a_re - 1.0) * li) / den
    bb_re = f_re[..., None] * b_re - f_im[..., None] * b_im
    bb_im = f_re[..., None] * b_im + f_im[..., None] * b_re
    bu_re = jnp.einsum('blgc,gpc->blgp', ug, bb_re)
    bu_im = jnp.einsum('blgc,gpc->blgp', ug, bb_im)
    shape_a = (1, seq, S5_GROUPS, S5_STATE)
    a_re_t = jnp.broadcast_to(a_re, shape_a)
    a_im_t = jnp.broadcast_to(a_im, shape_a)
    _, _, h_re, h_im = lax.associative_scan(
        _complex_affine_combine, (a_re_t, a_im_t, bu_re, bu_im), axis=1)
    y = (jnp.einsum('blgp,gcp->blgc', h_re, c_re)
         - jnp.einsum('blgp,gcp->blgc', h_im, c_im))
    y = y.reshape(bsz, seq, S5_WIDTH) + d_skip * u
    y = jax.nn.gelu(y)
    return y * jax.nn.sigmoid(y @ w_glu)


def _gla_mixer(q, k, v, r, gz, w_gate_up, b_gate, norm_g):
    bsz, seq, _ = q.shape
    f32 = jnp.float32
    n_chunks = seq // GLA_CHUNK
    q, k, v, r, gz = (t.astype(f32) for t in (q, k, v, r, gz))
    g = jax.nn.log_sigmoid(gz @ w_gate_up + b_gate) / GLA_TAU

    def chunks(t, d):
        return t.reshape(bsz, n_chunks, GLA_CHUNK, GLA_HEADS, d).transpose(0, 3, 1, 2, 4)

    qc = chunks(q, GLA_DK) * (GLA_DK ** -0.5)
    kc = chunks(k, GLA_DK)
    gc = chunks(g, GLA_DK)
    vc = chunks(v, GLA_DV)
    bcum = jnp.cumsum(gc, axis=3)
    blast = bcum[:, :, :, -1:, :]
    qe = qc * jnp.exp(bcum)
    ke = kc * jnp.exp(-bcum)
    mask = jnp.tril(jnp.ones((GLA_CHUNK, GLA_CHUNK), dtype=bool))
    scores = jnp.where(mask, jnp.einsum('bhnid,bhnjd->bhnij', qe, ke), 0.0)
    o_intra = jnp.einsum('bhnij,bhnje->bhnie', scores, vc)
    kd = kc * jnp.exp(blast - bcum)
    upd = jnp.einsum('bhncd,bhnce->nbhde', kd, vc)
    decay = jnp.exp(blast[:, :, :, 0, :]).transpose(2, 0, 1, 3)

    def step(state, inp):
        dec, u = inp
        return dec[..., None] * state + u, state

    s0 = jnp.zeros((bsz, GLA_HEADS, GLA_DK, GLA_DV), f32)
    _, s_prev = lax.scan(step, s0, (decay, upd))
    o_inter = jnp.einsum('bhncd,nbhde->bhnce', qe, s_prev)
    o = (o_intra + o_inter).transpose(0, 2, 3, 1, 4).reshape(bsz, seq, GLA_HEADS, GLA_DV)
    o = o * lax.rsqrt(jnp.mean(jnp.square(o), axis=-1, keepdims=True) + RMS_EPS) * norm_g
    return o.reshape(bsz, seq, GLA_WIDTH) * jax.nn.silu(r)


def _rglru_mixer(xb, gb, conv_w, conv_b, w_r, b_r, w_i, b_i, lam):
    bsz, seq, _ = xb.shape
    f32 = jnp.float32
    xb = xb.astype(f32)
    gb = gb.astype(f32)
    xc = lax.conv_general_dilated(
        xb, conv_w.astype(f32)[:, None, :], window_strides=(1,),
        padding=((LRU_CONV - 1, 0),), dimension_numbers=('NWC', 'WIO', 'NWC'),
        feature_group_count=LRU_WIDTH) + conv_b
    xblk = xc.reshape(bsz, seq, LRU_BLOCKS, LRU_BLOCK)
    gate_r = jax.nn.sigmoid(
        jnp.einsum('blhi,hij->blhj', xblk, w_r).reshape(bsz, seq, LRU_WIDTH) + b_r)
    gate_i = jax.nn.sigmoid(
        jnp.einsum('blhi,hij->blhj', xblk, w_i).reshape(bsz, seq, LRU_WIDTH) + b_i)
    log_a = -LRU_C * gate_r * jax.nn.softplus(-lam)
    a = jnp.exp(log_a)
    bterm = jnp.sqrt(-jnp.expm1(2.0 * log_a)) * (gate_i * xc)
    _, h = lax.associative_scan(_real_affine_combine, (a, bterm), axis=1)
    return h * jax.nn.gelu(gb)


def setup_inputs(seed: int = 0) -> dict:
    key = jax.random.key(seed)
    ks = jax.random.split(key, 32)
    f32 = jnp.float32
    L = DEPTH

    def nrm(k, shape, scale):
        return scale * jax.random.normal(k, shape, f32)

    x = jax.random.normal(ks[0], (BATCH, SEQ, D_MODEL), f32)
    ln_in_g = 1.0 + nrm(ks[1], (D_MODEL,), 0.02)
    ln_in_b = nrm(ks[2], (D_MODEL,), 0.02)
    w_in = nrm(ks[3], (L, D_MODEL, D_IN), D_MODEL ** -0.5)
    s5_lambda_re = -0.5 + nrm(ks[4], (L, S5_GROUPS, S5_STATE), 0.01)
    s5_lambda_im = (jnp.pi * jnp.arange(S5_STATE, dtype=f32)[None, None, :]
                    + nrm(ks[5], (L, S5_GROUPS, S5_STATE), 0.01))
    s5_log_dt = jax.random.uniform(ks[6], (L, S5_GROUPS), f32,
                                   minval=math.log(1e-3), maxval=math.log(1e-1))
    s5_b_re = nrm(ks[7], (L, S5_GROUPS, S5_STATE, S5_GROUP), (2 * S5_GROUP) ** -0.5)
    s5_b_im = nrm(ks[8], (L, S5_GROUPS, S5_STATE, S5_GROUP), (2 * S5_GROUP) ** -0.5)
    s5_c_re = nrm(ks[9], (L, S5_GROUPS, S5_GROUP, S5_STATE), (2 * S5_STATE) ** -0.5)
    s5_c_im = nrm(ks[10], (L, S5_GROUPS, S5_GROUP, S5_STATE), (2 * S5_STATE) ** -0.5)
    s5_d = nrm(ks[11], (L, S5_WIDTH), 1.0)
    s5_w_glu = nrm(ks[12], (L, S5_WIDTH, S5_WIDTH), S5_WIDTH ** -0.5)
    gla_w_gate_up = nrm(ks[13], (L, GLA_GATE_RANK, GLA_KEY_WIDTH), GLA_GATE_RANK ** -0.5)
    gla_b_gate = nrm(ks[14], (L, GLA_KEY_WIDTH), 0.02)
    gla_norm_g = 1.0 + nrm(ks[15], (L, GLA_DV), 0.02)
    lru_conv_w = nrm(ks[16], (L, LRU_CONV, LRU_WIDTH), LRU_CONV ** -0.5)
    lru_conv_b = nrm(ks[17], (L, LRU_WIDTH), 0.02)
    lru_w_r = nrm(ks[18], (L, LRU_BLOCKS, LRU_BLOCK, LRU_BLOCK), LRU_BLOCK ** -0.5)
    lru_b_r = nrm(ks[19], (L, LRU_WIDTH), 0.02)
    lru_w_i = nrm(ks[20], (L, LRU_BLOCKS, LRU_BLOCK, LRU_BLOCK), LRU_BLOCK ** -0.5)
    lru_b_i = nrm(ks[21], (L, LRU_WIDTH), 0.02)
    u = jax.random.uniform(ks[22], (L, LRU_WIDTH), f32, minval=0.9, maxval=0.999)
    p = u ** (1.0 / LRU_C)
    lru_lambda = jnp.log(p) - jnp.log1p(-p)
    w_out = nrm(ks[23], (L, D_MIX, D_MODEL), DEEPNORM_BETA * D_MIX ** -0.5)
    ln1_g = 1.0 + nrm(ks[24], (L, D_MODEL), 0.02)
    ln1_b = nrm(ks[25], (L, D_MODEL), 0.02)
    mlp_w1 = nrm(ks[26], (L, D_MODEL, D_FF), D_MODEL ** -0.5)
    mlp_w2 = nrm(ks[27], (L, D_FF, D_MODEL), DEEPNORM_BETA * D_FF ** -0.5)
    ln2_g = 1.0 + nrm(ks[28], (L, D_MODEL), 0.02)
    ln2_b = nrm(ks[29], (L, D_MODEL), 0.02)
    return {
        'x': x, 'ln_in_g': ln_in_g, 'ln_in_b': ln_in_b, 'w_in': w_in,
        's5_lambda_re': s5_lambda_re, 's5_lambda_im': s5_lambda_im, 's5_log_dt': s5_log_dt,
        's5_b_re': s5_b_re, 's5_b_im': s5_b_im, 's5_c_re': s5_c_re, 's5_c_im': s5_c_im,
        's5_d': s5_d, 's5_w_glu': s5_w_glu,
        'gla_w_gate_up': gla_w_gate_up, 'gla_b_gate': gla_b_gate, 'gla_norm_g': gla_norm_g,
        'lru_conv_w': lru_conv_w, 'lru_conv_b': lru_conv_b, 'lru_w_r': lru_w_r,
        'lru_b_r': lru_b_r, 'lru_w_i': lru_w_i, 'lru_b_i': lru_b_i, 'lru_lambda': lru_lambda,
        'w_out': w_out, 'ln1_g': ln1_g, 'ln1_b': ln1_b,
        'mlp_w1': mlp_w1, 'mlp_w2': mlp_w2, 'ln2_g': ln2_g, 'ln2_b': ln2_b,
    }


def reference(x, ln_in_g, ln_in_b, w_in, s5_lambda_re, s5_lambda_im, s5_log_dt,
              s5_b_re, s5_b_im, s5_c_re, s5_c_im, s5_d, s5_w_glu,
              gla_w_gate_up, gla_b_gate, gla_norm_g,
              lru_conv_w, lru_conv_b, lru_w_r, lru_b_r, lru_w_i, lru_b_i, lru_lambda,
              w_out, ln1_g, ln1_b, mlp_w1, mlp_w2, ln2_g, ln2_b):
    split_points = tuple(int(s) for s in np.cumsum(SPLIT_SIZES)[:-1])
    h = _layer_norm(x, ln_in_g, ln_in_b)
    for l in range(DEPTH):
        z = h @ w_in[l]
        s5_u, g_q, g_k, g_v, g_r, g_z, lru_x, lru_g = jnp.split(z, split_points, axis=-1)
        y_s5 = _s5_mixer(s5_u, s5_lambda_re[l], s5_lambda_im[l], s5_log_dt[l],
                         s5_b_re[l], s5_b_im[l], s5_c_re[l], s5_c_im[l], s5_d[l], s5_w_glu[l])
        y_gla = _gla_mixer(g_q, g_k, g_v, g_r, g_z, gla_w_gate_up[l], gla_b_gate[l],
                           gla_norm_g[l])
        y_lru = _rglru_mixer(lru_x, lru_g, lru_conv_w[l], lru_conv_b[l], lru_w_r[l],
                             lru_b_r[l], lru_w_i[l], lru_b_i[l], lru_lambda[l])
        mix = jnp.concatenate([y_s5, y_gla, y_lru], axis=-1) @ w_out[l]
        h = _layer_norm(DEEPNORM_ALPHA * h + mix, ln1_g[l], ln1_b[l])
        ff = jnp.square(jax.nn.relu(h @ mlp_w1[l])) @ mlp_w2[l]
        h = _layer_norm(DEEPNORM_ALPHA * h + ff, ln2_g[l], ln2_b[l])
    return h
```

```python
import functools
import math

import jax
import jax.numpy as jnp
from jax import lax
from jax.experimental import pallas as pl
from jax.experimental.pallas import tpu as pltpu

F32 = jnp.float32
BF16 = jnp.bfloat16

D_MODEL = 1024
BATCH = 8
SEQ = 8192
DEPTH = 2
S5_WIDTH = 256
S5_GROUP = 16
S5_GROUPS = 16
S5_STATE = 64
S5_MIN_DECAY = 1e-4
S5_NSTATE = S5_GROUPS * S5_STATE
GLA_WIDTH = 512
GLA_HEADS = 4
GLA_DV = 128
GLA_DK = 64
GLA_KEY_WIDTH = 256
GLA_GATE_RANK = 16
GLA_TAU = 16.0
GLA_CHUNK = 64
LRU_WIDTH = 256
LRU_BLOCKS = 8
LRU_BLOCK = 32
LRU_CONV = 4
LRU_C = 8.0
D_FF = 4 * D_MODEL
DEEPNORM_ALPHA = (2 * DEPTH) ** 0.25
LN_EPS = 1e-5
RMS_EPS = 1e-6

LANES = 128
SUBLANES = 8
TT = GLA_CHUNK
ROWS = TT * BATCH
HIST = (LRU_CONV - 1) * BATCH
GZ_PAD = LANES
OFF_S5 = 0
OFF_Q = OFF_S5 + S5_WIDTH
OFF_K = OFF_Q + GLA_KEY_WIDTH
OFF_V = OFF_K + GLA_KEY_WIDTH
OFF_R = OFF_V + GLA_WIDTH
OFF_LX = OFF_R + GLA_WIDTH
OFF_LG = OFF_LX + LRU_WIDTH
D_IN_MAIN = OFF_LG + LRU_WIDTH
SLAB_QE = 0
SLAB_KE = SLAB_QE + GLA_KEY_WIDTH // LANES
SLAB_KD = SLAB_KE + GLA_KEY_WIDTH // LANES
SLAB_V = SLAB_KD + GLA_KEY_WIDTH // LANES
SLAB_O = SLAB_V + GLA_WIDTH // LANES
N_SLABS = SLAB_O + GLA_WIDTH // LANES
S5_LANE_CHUNK = 512
MLP_ROWS = 512
MLP_FF_CHUNK = 1024
VMEM_LIMIT = 56 * 1024 * 1024


def _layer_norm(x, g, b):
    mu = jnp.mean(x, axis=-1, keepdims=True)
    xc = x - mu
    var = jnp.mean(xc * xc, axis=-1, keepdims=True)
    return xc * lax.rsqrt(var + LN_EPS) * g + b


def _dot(a, b):
    return jnp.dot(a, b, preferred_element_type=F32)


def _log_sigmoid(x):
    return jnp.minimum(x, 0.0) - jnp.log1p(jnp.exp(-jnp.abs(x)))


def _shift_rows(x, n):
    return jnp.concatenate([jnp.zeros((n, x.shape[1]), x.dtype), x[:-n]], axis=0)


def _mixer_kernel(apply_ln_in,
                  h_ref, lng_ref, lnb_ref, w_in_ref, w_gz_ref,
                  s5_a_ref, w_bu_ref, w_c_ref, s5_d_ref, w_glu_ref,
                  w_up_ref, b_gate_ref, norm_g_ref,
                  conv_w_ref, conv_b_ref, w_r_ref, b_r_ref, w_i_ref, b_i_ref, sp_ref,
                  w_out_ref, ln1g_ref, ln1b_ref,
                  out_ref,
                  s5_state_ref, gla_state_ref, lru_state_ref, xs_ref,
                  bu_ref, slab_ref, la_ref, lb_ref):
    step = pl.program_id(0)

    @pl.when(step == 0)
    def _():
        s5_state_ref[...] = jnp.zeros_like(s5_state_ref)
        gla_state_ref[...] = jnp.zeros_like(gla_state_ref)
        lru_state_ref[...] = jnp.zeros_like(lru_state_ref)
        xs_ref[0:HIST, :] = jnp.zeros((HIST, LRU_WIDTH), F32)

    h = h_ref[...].reshape(ROWS, D_MODEL)
    if apply_ln_in:
        h = _layer_norm(h, lng_ref[...], lnb_ref[...])
    hb = h.astype(BF16)

    u = _dot(hb, w_in_ref[:, OFF_S5:OFF_S5 + S5_WIDTH])
    bu_ref[...] = _dot(u.astype(BF16), w_bu_ref[...])
    for c in range(S5_NSTATE // S5_LANE_CHUNK):
        lo = c * S5_LANE_CHUNK
        re_sl = slice(lo, lo + S5_LANE_CHUNK)
        im_sl = slice(S5_NSTATE + lo, S5_NSTATE + lo + S5_LANE_CHUNK)
        a_re = jnp.broadcast_to(s5_a_ref[0:1, re_sl], (BATCH, S5_LANE_CHUNK))
        a_im = jnp.broadcast_to(s5_a_ref[0:1, im_sl], (BATCH, S5_LANE_CHUNK))

        def s5_step(t, carry):
            h_re, h_im = carry
            r0 = pl.multiple_of(t * BATCH, BATCH)
            b_re = bu_ref[pl.ds(r0, BATCH), re_sl]
            b_im = bu_ref[pl.ds(r0, BATCH), im_sl]
            n_re = a_re * h_re - a_im * h_im + b_re
            n_im = a_re * h_im + a_im * h_re + b_im
            bu_ref[pl.ds(r0, BATCH), re_sl] = n_re
            bu_ref[pl.ds(r0, BATCH), im_sl] = n_im
            return n_re, n_im

        h_re, h_im = lax.fori_loop(
            0, TT, s5_step, (s5_state_ref[:, re_sl], s5_state_ref[:, im_sl]), unroll=4)
        s5_state_ref[:, re_sl] = h_re
        s5_state_ref[:, im_sl] = h_im
    y = _dot(bu_ref[...].astype(BF16), w_c_ref[...]) + s5_d_ref[...] * u
    y = jax.nn.gelu(y)
    y_s5 = y * jax.nn.sigmoid(_dot(y.astype(BF16), w_glu_ref[...]))

    q = _dot(hb, w_in_ref[:, OFF_Q:OFF_Q + GLA_KEY_WIDTH]) * (GLA_DK ** -0.5)
    k = _dot(hb, w_in_ref[:, OFF_K:OFF_K + GLA_KEY_WIDTH])
    v = _dot(hb, w_in_ref[:, OFF_V:OFF_V + GLA_WIDTH])
    gz = _dot(hb, w_gz_ref[...])
    g = _log_sigmoid(_dot(gz.astype(BF16), w_up_ref[...]) + b_gate_ref[...]) * (1.0 / GLA_TAU)
    bcum = g
    sh = 1
    while sh < TT:
        bcum = bcum + _shift_rows(bcum, sh * BATCH)
        sh *= 2
    blast = bcum[ROWS - BATCH:ROWS, :]
    blast_rows = jnp.broadcast_to(blast[None], (TT, BATCH, GLA_KEY_WIDTH)).reshape(
        ROWS, GLA_KEY_WIDTH)
    qe = q * jnp.exp(bcum)
    ke = k * jnp.exp(-bcum)
    kd = k * jnp.exp(blast_rows - bcum)
    dec = jnp.exp(blast)
    for j in range(GLA_KEY_WIDTH // LANES):
        ls = slice(j * LANES, (j + 1) * LANES)
        slab_ref[SLAB_QE + j] = qe[:, ls]
        slab_ref[SLAB_KE + j] = ke[:, ls]
        slab_ref[SLAB_KD + j] = kd[:, ls]
    for j in range(GLA_WIDTH // LANES):
        slab_ref[SLAB_V + j] = v[:, j * LANES:(j + 1) * LANES]

    hc = GLA_HEADS * GLA_CHUNK
    row = lax.broadcasted_iota(jnp.int32, (hc, hc), 0)
    col = lax.broadcasted_iota(jnp.int32, (hc, hc), 1)
    head_mask = (row // GLA_CHUNK) == (col // GLA_DK)
    causal_mask = ((row // GLA_CHUNK) == (col // GLA_CHUNK)) & (
        (row % GLA_CHUNK) >= (col % GLA_CHUNK))
    norm_g = norm_g_ref[...]

    def gather(slab0, n, b):
        return jnp.concatenate(
            [slab_ref[slab0 + j, pl.ds(b, TT, stride=BATCH), :] for j in range(n)], axis=1)

    for b in range(BATCH):
        qe_b = gather(SLAB_QE, GLA_KEY_WIDTH // LANES, b)
        ke_b = gather(SLAB_KE, GLA_KEY_WIDTH // LANES, b)
        kd_b = gather(SLAB_KD, GLA_KEY_WIDTH // LANES, b)
        v_b = gather(SLAB_V, GLA_WIDTH // LANES, b)
        q_blk = jnp.where(head_mask, jnp.concatenate([qe_b] * GLA_HEADS, axis=0),
                          0.0).astype(BF16)
        k_blk = jnp.where(head_mask, jnp.concatenate([ke_b] * GLA_HEADS, axis=0),
                          0.0).astype(BF16)
        scores = lax.dot_general(q_blk, k_blk, (((1,), (1,)), ((), ())),
                                 preferred_element_type=F32)
        p = jnp.where(causal_mask, scores, 0.0).astype(BF16)
        v_stack = jnp.concatenate(
            [v_b[:, hh * GLA_DV:(hh + 1) * GLA_DV] for hh in range(GLA_HEADS)],
            axis=0).astype(BF16)
        s_b = gla_state_ref[b]
        o = _dot(p, v_stack) + _dot(q_blk, s_b.astype(BF16))
        o = o * lax.rsqrt(jnp.mean(o * o, axis=-1, keepdims=True) + RMS_EPS) * norm_g
        for hh in range(GLA_HEADS):
            slab_ref[SLAB_O + hh, pl.ds(b, TT, stride=BATCH), :] = (
                o[hh * GLA_CHUNK:(hh + 1) * GLA_CHUNK, :])
        x = jnp.concatenate(
            [kd_b, jnp.broadcast_to(dec[b:b + 1, :], (SUBLANES, GLA_KEY_WIDTH)),
             jnp.zeros((LANES - GLA_CHUNK - SUBLANES, GLA_KEY_WIDTH), F32)], axis=0)
        xt = x.T
        kd_t = xt[:, 0:GLA_CHUNK].astype(BF16)
        dec_col = xt[:, GLA_CHUNK:GLA_CHUNK + 1]
        m = _dot(kd_t, v_b.astype(BF16))
        upd = jnp.concatenate(
            [m[hh * GLA_DK:(hh + 1) * GLA_DK, hh * GLA_DV:(hh + 1) * GLA_DV]
             for hh in range(GLA_HEADS)], axis=0)
        gla_state_ref[b] = dec_col * s_b + upd

    o_all = jnp.concatenate([slab_ref[SLAB_O + j] for j in range(GLA_WIDTH // LANES)], axis=1)
    r = _dot(hb, w_in_ref[:, OFF_R:OFF_R + GLA_WIDTH])
    y_gla = o_all * (r * jax.nn.sigmoid(r))

    xl = _dot(hb, w_in_ref[:, OFF_LX:OFF_LX + LRU_WIDTH])
    gb = _dot(hb, w_in_ref[:, OFF_LG:OFF_LG + LRU_WIDTH])
    xs_ref[HIST:HIST + ROWS, :] = xl
    xc = conv_b_ref[...] + conv_w_ref[LRU_CONV - 1:LRU_CONV, :] * xl
    for kk in range(LRU_CONV - 1):
        xc = xc + conv_w_ref[kk:kk + 1, :] * xs_ref[kk * BATCH:kk * BATCH + ROWS, :]
    xs_ref[0:HIST, :] = xl[ROWS - HIST:ROWS, :]
    xcb = xc.astype(BF16)
    gate_r = jax.nn.sigmoid(_dot(xcb, w_r_ref[...]) + b_r_ref[...])
    gate_i = jax.nn.sigmoid(_dot(xcb, w_i_ref[...]) + b_i_ref[...])
    log_a = -LRU_C * gate_r * sp_ref[...]
    la_ref[...] = jnp.exp(log_a)
    th = jnp.tanh(log_a)
    lb_ref[...] = jnp.sqrt(-2.0 * th / (1.0 - th)) * (gate_i * xc)

    def lru_step(t, hl):
        r0 = pl.multiple_of(t * BATCH, BATCH)
        hl = la_ref[pl.ds(r0, BATCH), :] * hl + lb_ref[pl.ds(r0, BATCH), :]
        lb_ref[pl.ds(r0, BATCH), :] = hl
        return hl

    lru_state_ref[...] = lax.fori_loop(0, TT, lru_step, lru_state_ref[...], unroll=8)
    y_lru = lb_ref[...] * jax.nn.gelu(gb)

    mix = (_dot(y_s5.astype(BF16), w_out_ref[0:S5_WIDTH, :])
           + _dot(y_gla.astype(BF16), w_out_ref[S5_WIDTH:S5_WIDTH + GLA_WIDTH, :])
           + _dot(y_lru.astype(BF16), w_out_ref[S5_WIDTH + GLA_WIDTH:, :]))
    h1 = _layer_norm(DEEPNORM_ALPHA * h + mix, ln1g_ref[...], ln1b_ref[...])
    out_ref[...] = h1.reshape(TT, BATCH, D_MODEL)


def _mlp_kernel(h_ref, w1_ref, w2_ref, g_ref, b_ref, out_ref):
    h = h_ref[...]
    hb = h.astype(BF16)
    acc = jnp.zeros((MLP_ROWS, D_MODEL), F32)
    for c in range(D_FF // MLP_FF_CHUNK):
        cs = slice(c * MLP_FF_CHUNK, (c + 1) * MLP_FF_CHUNK)
        a = jnp.maximum(_dot(hb, w1_ref[:, cs]), 0.0)
        acc = acc + _dot((a * a).astype(BF16), w2_ref[cs, :])
    out_ref[...] = _layer_norm(DEEPNORM_ALPHA * h + acc, g_ref[...], b_ref[...])


def _const_spec(shape):
    nd = len(shape)
    return pl.BlockSpec(shape, lambda i, _nd=nd: (0,) * _nd)


def _mixer_call(apply_ln_in, h, weights):
    in_specs = [pl.BlockSpec((TT, BATCH, D_MODEL), lambda i: (i, 0, 0))]
    in_specs += [_const_spec(w.shape) for w in weights]
    return pl.pallas_call(
        functools.partial(_mixer_kernel, apply_ln_in),
        grid=(SEQ // TT,),
        in_specs=in_specs,
        out_specs=pl.BlockSpec((TT, BATCH, D_MODEL), lambda i: (i, 0, 0)),
        out_shape=jax.ShapeDtypeStruct((SEQ, BATCH, D_MODEL), F32),
        scratch_shapes=[
            pltpu.VMEM((BATCH, 2 * S5_NSTATE), F32),
            pltpu.VMEM((BATCH, GLA_HEADS * GLA_DK, GLA_DV), F32),
            pltpu.VMEM((BATCH, LRU_WIDTH), F32),
            pltpu.VMEM((HIST + ROWS, LRU_WIDTH), F32),
            pltpu.VMEM((ROWS, 2 * S5_NSTATE), F32),
            pltpu.VMEM((N_SLABS, ROWS, LANES), F32),
            pltpu.VMEM((ROWS, LRU_WIDTH), F32),
            pltpu.VMEM((ROWS, LRU_WIDTH), F32),
        ],
        compiler_params=pltpu.CompilerParams(
            dimension_semantics=("arbitrary",), vmem_limit_bytes=VMEM_LIMIT),
        name="mixer",
    )(h, *weights)


def _mlp_call(h2d, w1, w2, g, b):
    n = h2d.shape[0]
    return pl.pallas_call(
        _mlp_kernel,
        grid=(n // MLP_ROWS,),
        in_specs=[pl.BlockSpec((MLP_ROWS, D_MODEL), lambda i: (i, 0)),
                  _const_spec(w1.shape), _const_spec(w2.shape),
                  _const_spec(g.shape), _const_spec(b.shape)],
        out_specs=pl.BlockSpec((MLP_ROWS, D_MODEL), lambda i: (i, 0)),
        out_shape=jax.ShapeDtypeStruct((n, D_MODEL), F32),
        compiler_params=pltpu.CompilerParams(
            dimension_semantics=("arbitrary",), vmem_limit_bytes=VMEM_LIMIT),
        name="mlp",
    )(h2d, w1, w2, g, b)


def _block_diag(w):
    n, a, b = w.shape
    eye = jnp.eye(n, dtype=w.dtype)
    return jnp.einsum('gab,gh->gahb', w, eye).reshape(n * a, n * b)


def _s5_params(lam_re, lam_im, log_dt, b_re, b_im, c_re, c_im):
    dt = jnp.exp(log_dt)[:, None]
    lr = jnp.minimum(lam_re, -S5_MIN_DECAY)
    li = lam_im
    mag = jnp.exp(lr * dt)
    a_re = mag * jnp.cos(li * dt)
    a_im = mag * jnp.sin(li * dt)
    den = lr * lr + li * li
    f_re = ((a_re - 1.0) * lr + a_im * li) / den
    f_im = (a_im * lr - (a_re - 1.0) * li) / den
    bb_re = f_re[..., None] * b_re - f_im[..., None] * b_im
    bb_im = f_re[..., None] * b_im + f_im[..., None] * b_re
    a = jnp.concatenate([a_re.reshape(1, -1), a_im.reshape(1, -1)], axis=1)
    w_bu = jnp.concatenate([_block_diag(bb_re.transpose(0, 2, 1)),
                            _block_diag(bb_im.transpose(0, 2, 1))], axis=1)
    w_c = jnp.concatenate([_block_diag(c_re.transpose(0, 2, 1)),
                           -_block_diag(c_im.transpose(0, 2, 1))], axis=0)
    return a, w_bu.astype(BF16), w_c.astype(BF16)


def kernel(x, ln_in_g, ln_in_b, w_in, s5_lambda_re, s5_lambda_im, s5_log_dt, s5_b_re, s5_b_im, s5_c_re, s5_c_im, s5_d, s5_w_glu, gla_w_gate_up, gla_b_gate, gla_norm_g, lru_conv_w, lru_conv_b, lru_w_r, lru_b_r, lru_w_i, lru_b_i, lru_lambda, w_out, ln1_g, ln1_b, mlp_w1, mlp_w2, ln2_g, ln2_b):
    assert x.shape == (BATCH, SEQ, D_MODEL)
    row = lambda t: t.reshape(1, -1).astype(F32)
    gz0 = OFF_R + GLA_WIDTH
    h = jnp.transpose(x, (1, 0, 2))
    for l in range(DEPTH):
        w_main = jnp.concatenate(
            [w_in[l][:, :gz0], w_in[l][:, gz0 + GLA_GATE_RANK:]], axis=1).astype(BF16)
        w_gz = jnp.pad(w_in[l][:, gz0:gz0 + GLA_GATE_RANK],
                       ((0, 0), (0, GZ_PAD - GLA_GATE_RANK))).astype(BF16)
        w_up = jnp.pad(gla_w_gate_up[l], ((0, GZ_PAD - GLA_GATE_RANK), (0, 0))).astype(BF16)
        s5_a, w_bu, w_c = _s5_params(s5_lambda_re[l], s5_lambda_im[l], s5_log_dt[l],
                                     s5_b_re[l], s5_b_im[l], s5_c_re[l], s5_c_im[l])
        weights = [
            row(ln_in_g), row(ln_in_b), w_main, w_gz,
            s5_a, w_bu, w_c, row(s5_d[l]), s5_w_glu[l].astype(BF16),
            w_up, row(gla_b_gate[l]), row(gla_norm_g[l]),
            lru_conv_w[l].astype(F32), row(lru_conv_b[l]),
            _block_diag(lru_w_r[l]).astype(BF16), row(lru_b_r[l]),
            _block_diag(lru_w_i[l]).astype(BF16), row(lru_b_i[l]),
            row(jax.nn.softplus(-lru_lambda[l])),
            w_out[l].astype(BF16), row(ln1_g[l]), row(ln1_b[l]),
        ]
        h = _mixer_call(l == 0, h, weights)
        h = _mlp_call(h.reshape(SEQ * BATCH, D_MODEL), mlp_w1[l].astype(BF16),
                      mlp_w2[l].astype(BF16), row(ln2_g[l]), row(ln2_b[l]))
        h = h.reshape(SEQ, BATCH, D_MODEL)
    return jnp.transpose(h, (1, 0, 2))
```

```python
import functools
import math

import jax
import jax.numpy as jnp
from jax import lax
from jax.experimental import pallas as pl
from jax.experimental.pallas import tpu as pltpu

F32 = jnp.float32
BF16 = jnp.bfloat16

D_MODEL = 1024
BATCH = 8
SEQ = 8192
DEPTH = 2
S5_WIDTH = 256
S5_GROUP = 16
S5_GROUPS = 16
S5_STATE = 64
S5_MIN_DECAY = 1e-4
S5_NSTATE = S5_GROUPS * S5_STATE
GLA_WIDTH = 512
GLA_HEADS = 4
GLA_DV = 128
GLA_DK = 64
GLA_KEY_WIDTH = 256
GLA_GATE_RANK = 16
GLA_TAU = 16.0
GLA_CHUNK = 64
LRU_WIDTH = 256
LRU_BLOCKS = 8
LRU_BLOCK = 32
LRU_CONV = 4
LRU_C = 8.0
D_FF = 4 * D_MODEL
DEEPNORM_ALPHA = (2 * DEPTH) ** 0.25
LN_EPS = 1e-5
RMS_EPS = 1e-6

LANES = 128
SUBLANES = 8
TT = GLA_CHUNK
ROWS = TT * BATCH
HIST = (LRU_CONV - 1) * BATCH
GZ_PAD = LANES
OFF_S5 = 0
OFF_Q = OFF_S5 + S5_WIDTH
OFF_K = OFF_Q + GLA_KEY_WIDTH
OFF_V = OFF_K + GLA_KEY_WIDTH
OFF_R = OFF_V + GLA_WIDTH
OFF_LX = OFF_R + GLA_WIDTH
OFF_LG = OFF_LX + LRU_WIDTH
D_IN_MAIN = OFF_LG + LRU_WIDTH
SLAB_QE = 0
SLAB_KE = SLAB_QE + GLA_KEY_WIDTH // LANES
SLAB_KD = SLAB_KE + GLA_KEY_WIDTH // LANES
SLAB_V = SLAB_KD + GLA_KEY_WIDTH // LANES
N_SLABS = SLAB_V + GLA_WIDTH // LANES
S5_LANE_CHUNK = 512
MLP_ROWS = 1024
MLP_FF_CHUNK = 1024
VMEM_LIMIT = 56 * 1024 * 1024


def _layer_norm(x, g, b):
    mu = jnp.mean(x, axis=-1, keepdims=True)
    xc = x - mu
    var = jnp.mean(xc * xc, axis=-1, keepdims=True)
    return xc * lax.rsqrt(var + LN_EPS) * g + b


_dot = functools.partial(jnp.dot, preferred_element_type=F32)


def _log_sigmoid(x):
    return jnp.minimum(x, 0.0) - jnp.log(1.0 + jnp.exp(-jnp.abs(x)))


def _shift_rows(x, n):
    return jnp.concatenate([jnp.zeros((n, x.shape[1]), x.dtype), x[:-n]], axis=0)


def _mixer_kernel(apply_ln_in,
                  h_ref, lng_ref, lnb_ref, w_in_ref, w_gz_ref,
                  s5_a_ref, w_bu_ref, w_c_ref, s5_d_ref, w_glu_ref,
                  w_up_ref, b_gate_ref, norm_g_ref,
                  conv_w_ref, conv_b_ref, w_r_ref, b_r_ref, w_i_ref, b_i_ref, sp_ref,
                  w_out_ref, ln1g_ref, ln1b_ref,
                  out_ref,
                  s5_state_ref, gla_state_ref, lru_state_ref, xs_ref,
                  bu_ref, hs_ref, slab_ref, oslab_ref, la_ref, lb_ref):
    step = pl.program_id(0)

    @pl.when(step == 0)
    def _():
        s5_state_ref[...] = jnp.zeros_like(s5_state_ref)
        gla_state_ref[...] = jnp.zeros_like(gla_state_ref)
        lru_state_ref[...] = jnp.zeros_like(lru_state_ref)
        xs_ref[0:HIST, :] = jnp.zeros((HIST, LRU_WIDTH), F32)

    h = h_ref[...].reshape(ROWS, D_MODEL)
    if apply_ln_in:
        h = _layer_norm(h, lng_ref[...], lnb_ref[...])
    hb = h.astype(BF16)

    u = _dot(hb, w_in_ref[:, OFF_S5:OFF_S5 + S5_WIDTH])
    q = _dot(hb, w_in_ref[:, OFF_Q:OFF_Q + GLA_KEY_WIDTH]) * (GLA_DK ** -0.5)
    k = _dot(hb, w_in_ref[:, OFF_K:OFF_K + GLA_KEY_WIDTH])
    bu_ref[...] = _dot(u.astype(BF16), w_bu_ref[...])
    v = _dot(hb, w_in_ref[:, OFF_V:OFF_V + GLA_WIDTH])
    gz = _dot(hb, w_gz_ref[...])
    xl = _dot(hb, w_in_ref[:, OFF_LX:OFF_LX + LRU_WIDTH])

    for c in range(S5_NSTATE // S5_LANE_CHUNK):
        lo = c * S5_LANE_CHUNK
        re_sl = slice(lo, lo + S5_LANE_CHUNK)
        im_sl = slice(S5_NSTATE + lo, S5_NSTATE + lo + S5_LANE_CHUNK)
        a_re = jnp.broadcast_to(s5_a_ref[0:1, re_sl], (BATCH, S5_LANE_CHUNK))
        a_im = jnp.broadcast_to(s5_a_ref[0:1, im_sl], (BATCH, S5_LANE_CHUNK))
        h_re = s5_state_ref[:, re_sl]
        h_im = s5_state_ref[:, im_sl]
        for t in range(0, TT, 2):
            pair_re, pair_im = [], []
            for tt in (t, t + 1):
                rows = slice(tt * BATCH, (tt + 1) * BATCH)
                n_re = a_re * h_re - a_im * h_im + bu_ref[rows, re_sl]
                n_im = a_re * h_im + a_im * h_re + bu_ref[rows, im_sl]
                h_re, h_im = n_re, n_im
                pair_re.append(n_re)
                pair_im.append(n_im)
            rows2 = slice(t * BATCH, (t + 2) * BATCH)
            hs_ref[rows2, re_sl] = jnp.concatenate(pair_re, axis=0).astype(BF16)
            hs_ref[rows2, im_sl] = jnp.concatenate(pair_im, axis=0).astype(BF16)
        s5_state_ref[:, re_sl] = h_re
        s5_state_ref[:, im_sl] = h_im

    g_pre = _dot(gz.astype(BF16), w_up_ref[...]) + b_gate_ref[...]
    gb = _dot(hb, w_in_ref[:, OFF_LG:OFF_LG + LRU_WIDTH])
    r = _dot(hb, w_in_ref[:, OFF_R:OFF_R + GLA_WIDTH])

    xs_ref[HIST:HIST + ROWS, :] = xl
    xc = conv_b_ref[...] + conv_w_ref[LRU_CONV - 1:LRU_CONV, :] * xl
    for kk in range(LRU_CONV - 1):
        xc = xc + conv_w_ref[kk:kk + 1, :] * xs_ref[kk * BATCH:kk * BATCH + ROWS, :]
    xs_ref[0:HIST, :] = xl[ROWS - HIST:ROWS, :]
    xcb = xc.astype(BF16)
    gate_r = jax.nn.sigmoid(_dot(xcb, w_r_ref[...]) + b_r_ref[...])
    gate_i = jax.nn.sigmoid(_dot(xcb, w_i_ref[...]) + b_i_ref[...])
    log_a = -LRU_C * gate_r * sp_ref[...]
    la_ref[...] = jnp.exp(log_a)
    th = jnp.tanh(log_a)
    lb_ref[...] = jnp.sqrt(-2.0 * th / (1.0 - th)) * (gate_i * xc)

    g = _log_sigmoid(g_pre) * (1.0 / GLA_TAU)
    bcum = g
    sh = 1
    while sh < TT:
        bcum = bcum + _shift_rows(bcum, sh * BATCH)
        sh *= 2
    blast = bcum[ROWS - BATCH:ROWS, :]
    blast_rows = jnp.broadcast_to(blast[None], (TT, BATCH, GLA_KEY_WIDTH)).reshape(
        ROWS, GLA_KEY_WIDTH)
    qe = q * jnp.exp(bcum)
    ke = k * jnp.exp(-bcum)
    kd = k * jnp.exp(blast_rows - bcum)
    dec = jnp.exp(blast)
    for j in range(GLA_KEY_WIDTH // LANES):
        ls = slice(j * LANES, (j + 1) * LANES)
        slab_ref[SLAB_QE + j] = qe[:, ls]
        slab_ref[SLAB_KE + j] = ke[:, ls]
        slab_ref[SLAB_KD + j] = kd[:, ls]
    for j in range(GLA_WIDTH // LANES):
        slab_ref[SLAB_V + j] = v[:, j * LANES:(j + 1) * LANES]

    y = jnp.concatenate(
        [_dot(hs_ref[0:ROWS // 2, :], w_c_ref[...]), _dot(hs_ref[ROWS // 2:, :], w_c_ref[...])],
        axis=0) + s5_d_ref[...] * u
    y = jax.nn.gelu(y)
    y_s5 = y * jax.nn.sigmoid(_dot(y.astype(BF16), w_glu_ref[...]))
    mix = _dot(y_s5.astype(BF16), w_out_ref[0:S5_WIDTH, :])

    hl = lru_state_ref[...]
    for t in range(TT):
        rows = slice(t * BATCH, (t + 1) * BATCH)
        hl = la_ref[rows, :] * hl + lb_ref[rows, :]
        lb_ref[rows, :] = hl
    lru_state_ref[...] = hl
    y_lru = lb_ref[...] * jax.nn.gelu(gb)
    mix = mix + _dot(y_lru.astype(BF16), w_out_ref[S5_WIDTH + GLA_WIDTH:, :])


    hc = GLA_HEADS * GLA_CHUNK
    row = lax.broadcasted_iota(jnp.int32, (hc, hc), 0)
    col = lax.broadcasted_iota(jnp.int32, (hc, hc), 1)
    head_mask = (row // GLA_CHUNK) == (col // GLA_DK)
    causal_mask = ((row // GLA_CHUNK) == (col // GLA_CHUNK)) & (
        (row % GLA_CHUNK) >= (col % GLA_CHUNK))
    norm_g = norm_g_ref[...]

    def gather(slab0, n, b):
        return jnp.concatenate(
            [slab_ref[slab0 + j, pl.ds(b, TT, stride=BATCH), :] for j in range(n)], axis=1)

    for b in range(BATCH):
        qe_b = gather(SLAB_QE, GLA_KEY_WIDTH // LANES, b)
        ke_b = gather(SLAB_KE, GLA_KEY_WIDTH // LANES, b)
        kd_b = gather(SLAB_KD, GLA_KEY_WIDTH // LANES, b)
        v_b = gather(SLAB_V, GLA_WIDTH // LANES, b)
        q_blk = jnp.where(head_mask, jnp.concatenate([qe_b] * GLA_HEADS, axis=0),
                          0.0).astype(BF16)
        k_blk = jnp.where(head_mask, jnp.concatenate([ke_b] * GLA_HEADS, axis=0),
                          0.0).astype(BF16)
        scores = lax.dot_general(q_blk, k_blk, (((1,), (1,)), ((), ())),
                                 preferred_element_type=F32)
        p = jnp.where(causal_mask, scores, 0.0).astype(BF16)
        v_stack = jnp.concatenate(
            [v_b[:, hh * GLA_DV:(hh + 1) * GLA_DV] for hh in range(GLA_HEADS)],
            axis=0).astype(BF16)
        s_b = gla_state_ref[b]
        o = _dot(p, v_stack) + _dot(q_blk, s_b.astype(BF16))
        o = o * lax.rsqrt(jnp.mean(o * o, axis=-1, keepdims=True) + RMS_EPS) * norm_g
        for hh in range(GLA_HEADS):
            oslab_ref[hh, pl.ds(b, TT, stride=BATCH), :] = (
                o[hh * GLA_CHUNK:(hh + 1) * GLA_CHUNK, :])
        x = jnp.concatenate(
            [kd_b, jnp.broadcast_to(dec[b:b + 1, :], (SUBLANES, GLA_KEY_WIDTH)),
             jnp.zeros((LANES - GLA_CHUNK - SUBLANES, GLA_KEY_WIDTH), F32)], axis=0)
        xt = x.T
        kd_t = xt[:, 0:GLA_CHUNK].astype(BF16)
        dec_col = xt[:, GLA_CHUNK:GLA_CHUNK + 1]
        m = _dot(kd_t, v_b.astype(BF16))
        upd = jnp.concatenate(
            [m[hh * GLA_DK:(hh + 1) * GLA_DK, hh * GLA_DV:(hh + 1) * GLA_DV]
             for hh in range(GLA_HEADS)], axis=0)
        gla_state_ref[b] = dec_col * s_b + upd

    o_all = jnp.concatenate([oslab_ref[j] for j in range(GLA_WIDTH // LANES)], axis=1)
    y_gla = o_all * (r * jax.nn.sigmoid(r))
    mix = mix + _dot(y_gla.astype(BF16), w_out_ref[S5_WIDTH:S5_WIDTH + GLA_WIDTH, :])

    h1 = _layer_norm(DEEPNORM_ALPHA * h + mix, ln1g_ref[...], ln1b_ref[...])
    out_ref[...] = h1.reshape(TT, BATCH, D_MODEL)


def _mlp_kernel(h_ref, w1_ref, w2_ref, g_ref, b_ref, out_ref):
    h = h_ref[...]
    hb = h.astype(BF16)
    acc = jnp.zeros((MLP_ROWS, D_MODEL), F32)
    for c in range(D_FF // MLP_FF_CHUNK):
        cs = slice(c * MLP_FF_CHUNK, (c + 1) * MLP_FF_CHUNK)
        a = jnp.maximum(_dot(hb, w1_ref[:, cs]), 0.0)
        acc = acc + _dot((a * a).astype(BF16), w2_ref[cs, :])
    out_ref[...] = _layer_norm(DEEPNORM_ALPHA * h + acc, g_ref[...], b_ref[...])


def _const_spec(shape):
    nd = len(shape)
    return pl.BlockSpec(shape, lambda i, _nd=nd: (0,) * _nd)


def _mixer_call(apply_ln_in, h, weights):
    in_specs = [pl.BlockSpec((TT, BATCH, D_MODEL), lambda i: (i, 0, 0))]
    in_specs += [_const_spec(w.shape) for w in weights]
    return pl.pallas_call(
        functools.partial(_mixer_kernel, apply_ln_in),
        grid=(SEQ // TT,),
        in_specs=in_specs,
        out_specs=pl.BlockSpec((TT, BATCH, D_MODEL), lambda i: (i, 0, 0)),
        out_shape=jax.ShapeDtypeStruct((SEQ, BATCH, D_MODEL), F32),
        scratch_shapes=[
            pltpu.VMEM((BATCH, 2 * S5_NSTATE), F32),
            pltpu.VMEM((BATCH, GLA_HEADS * GLA_DK, GLA_DV), F32),
            pltpu.VMEM((BATCH, LRU_WIDTH), F32),
            pltpu.VMEM((HIST + ROWS, LRU_WIDTH), F32),
            pltpu.VMEM((ROWS, 2 * S5_NSTATE), F32),
            pltpu.VMEM((ROWS, 2 * S5_NSTATE), BF16),
            pltpu.VMEM((N_SLABS, ROWS, LANES), F32),
            pltpu.VMEM((GLA_WIDTH // LANES, ROWS, LANES), F32),
            pltpu.VMEM((ROWS, LRU_WIDTH), F32),
            pltpu.VMEM((ROWS, LRU_WIDTH), F32),
        ],
        compiler_params=pltpu.CompilerParams(
            dimension_semantics=("arbitrary",), vmem_limit_bytes=VMEM_LIMIT),
        name="mixer",
    )(h, *weights)


def _mlp_call(h2d, w1, w2, g, b):
    n = h2d.shape[0]
    return pl.pallas_call(
        _mlp_kernel,
        grid=(n // MLP_ROWS,),
        in_specs=[pl.BlockSpec((MLP_ROWS, D_MODEL), lambda i: (i, 0)),
                  _const_spec(w1.shape), _const_spec(w2.shape),
                  _const_spec(g.shape), _const_spec(b.shape)],
        out_specs=pl.BlockSpec((MLP_ROWS, D_MODEL), lambda i: (i, 0)),
        out_shape=jax.ShapeDtypeStruct((n, D_MODEL), F32),
        compiler_params=pltpu.CompilerParams(
            dimension_semantics=("arbitrary",), vmem_limit_bytes=VMEM_LIMIT),
        name="mlp",
    )(h2d, w1, w2, g, b)


def _block_diag(w):
    n, a, b = w.shape
    eye = jnp.eye(n, dtype=w.dtype)
    return jnp.einsum('gab,gh->gahb', w, eye).reshape(n * a, n * b)


def _s5_params(lam_re, lam_im, log_dt, b_re, b_im, c_re, c_im):
    dt = jnp.exp(log_dt)[:, None]
    lr = jnp.minimum(lam_re, -S5_MIN_DECAY)
    li = lam_im
    mag = jnp.exp(lr * dt)
    a_re = mag * jnp.cos(li * dt)
    a_im = mag * jnp.sin(li * dt)
    den = lr * lr + li * li
    f_re = ((a_re - 1.0) * lr + a_im * li) / den
    f_im = (a_im * lr - (a_re - 1.0) * li) / den
    bb_re = f_re[..., None] * b_re - f_im[..., None] * b_im
    bb_im = f_re[..., None] * b_im + f_im[..., None] * b_re
    a = jnp.concatenate([a_re.reshape(1, -1), a_im.reshape(1, -1)], axis=1)
    w_bu = jnp.concatenate([_block_diag(bb_re.transpose(0, 2, 1)),
                            _block_diag(bb_im.transpose(0, 2, 1))], axis=1)
    w_c = jnp.concatenate([_block_diag(c_re.transpose(0, 2, 1)),
                           -_block_diag(c_im.transpose(0, 2, 1))], axis=0)
    return a, w_bu.astype(BF16), w_c.astype(BF16)


def kernel(x, ln_in_g, ln_in_b, w_in, s5_lambda_re, s5_lambda_im, s5_log_dt, s5_b_re, s5_b_im, s5_c_re, s5_c_im, s5_d, s5_w_glu, gla_w_gate_up, gla_b_gate, gla_norm_g, lru_conv_w, lru_conv_b, lru_w_r, lru_b_r, lru_w_i, lru_b_i, lru_lambda, w_out, ln1_g, ln1_b, mlp_w1, mlp_w2, ln2_g, ln2_b):
    assert x.shape == (BATCH, SEQ, D_MODEL)
    row = lambda t: t.reshape(1, -1).astype(F32)
    gz0 = OFF_R + GLA_WIDTH
    h = jnp.transpose(x, (1, 0, 2))
    for l in range(DEPTH):
        w_main = jnp.concatenate(
            [w_in[l][:, :gz0], w_in[l][:, gz0 + GLA_GATE_RANK:]], axis=1).astype(BF16)
        w_gz = jnp.pad(w_in[l][:, gz0:gz0 + GLA_GATE_RANK],
                       ((0, 0), (0, GZ_PAD - GLA_GATE_RANK))).astype(BF16)
        w_up = jnp.pad(gla_w_gate_up[l], ((0, GZ_PAD - GLA_GATE_RANK), (0, 0))).astype(BF16)
        s5_a, w_bu, w_c = _s5_params(s5_lambda_re[l], s5_lambda_im[l], s5_log_dt[l],
                                     s5_b_re[l], s5_b_im[l], s5_c_re[l], s5_c_im[l])
        weights = [
            row(ln_in_g), row(ln_in_b), w_main, w_gz,
            s5_a, w_bu, w_c, row(s5_d[l]), s5_w_glu[l].astype(BF16),
            w_up, row(gla_b_gate[l]), row(gla_norm_g[l]),
            lru_conv_w[l].astype(F32), row(lru_conv_b[l]),
            _block_diag(lru_w_r[l]).astype(BF16), row(lru_b_r[l]),
            _block_diag(lru_w_i[l]).astype(BF16), row(lru_b_i[l]),
            row(jax.nn.softplus(-lru_lambda[l])),
            w_out[l].astype(BF16), row(ln1_g[l]), row(ln1_b[l]),
        ]
        h = _mixer_call(l == 0, h, weights)
        h = _mlp_call(h.reshape(SEQ * BATCH, D_MODEL), mlp_w1[l].astype(BF16),
                      mlp_w2[l].astype(BF16), row(ln2_g[l]), row(ln2_b[l]))
        h = h.reshape(SEQ, BATCH, D_MODEL)
    return jnp.transpose(h, (1, 0, 2))
```

```python
import functools
import math

import jax
import jax.numpy as jnp
from jax import lax
from jax.experimental import pallas as pl
from jax.experimental.pallas import tpu as pltpu

F32 = jnp.float32
BF16 = jnp.bfloat16

D_MODEL = 1024
BATCH = 8
SEQ = 8192
DEPTH = 2
S5_WIDTH = 256
S5_GROUP = 16
S5_GROUPS = 16
S5_STATE = 64
S5_MIN_DECAY = 1e-4
S5_NSTATE = S5_GROUPS * S5_STATE
GLA_WIDTH = 512
GLA_HEADS = 4
GLA_DV = 128
GLA_DK = 64
GLA_KEY_WIDTH = 256
GLA_GATE_RANK = 16
GLA_TAU = 16.0
GLA_CHUNK = 64
LRU_WIDTH = 256
LRU_BLOCKS = 8
LRU_BLOCK = 32
LRU_CONV = 4
LRU_C = 8.0
D_FF = 4 * D_MODEL
DEEPNORM_ALPHA = (2 * DEPTH) ** 0.25
LN_EPS = 1e-5
RMS_EPS = 1e-6

LANES = 128
SUBLANES = 8
TT = GLA_CHUNK
ROWS = TT * BATCH
HIST = (LRU_CONV - 1) * BATCH
GZ_PAD = LANES
OFF_S5 = 0
OFF_Q = OFF_S5 + S5_WIDTH
OFF_K = OFF_Q + GLA_KEY_WIDTH
OFF_V = OFF_K + GLA_KEY_WIDTH
OFF_R = OFF_V + GLA_WIDTH
OFF_LX = OFF_R + GLA_WIDTH
OFF_LG = OFF_LX + LRU_WIDTH
D_IN_MAIN = OFF_LG + LRU_WIDTH
SLAB_QE = 0
SLAB_KE = SLAB_QE + GLA_KEY_WIDTH // LANES
SLAB_KD = SLAB_KE + GLA_KEY_WIDTH // LANES
SLAB_V = SLAB_KD + GLA_KEY_WIDTH // LANES
N_SLABS = SLAB_V + GLA_WIDTH // LANES
S5_LANE_CHUNK = 512
MLP_ROWS = 1024
MLP_TT = MLP_ROWS // BATCH
MLP_FF_CHUNK = 1024
VMEM_LIMIT = 56 * 1024 * 1024


def _layer_norm(x, g, b):
    mu = jnp.mean(x, axis=-1, keepdims=True)
    xc = x - mu
    var = jnp.mean(xc * xc, axis=-1, keepdims=True)
    return xc * lax.rsqrt(var + LN_EPS) * g + b


_dot = functools.partial(jnp.dot, preferred_element_type=F32)


def _log_sigmoid(x):
    return jnp.minimum(x, 0.0) - jnp.log(1.0 + jnp.exp(-jnp.abs(x)))


def _shift_rows(x, n):
    return jnp.concatenate([jnp.zeros((n, x.shape[1]), x.dtype), x[:-n]], axis=0)


def _mixer_kernel(first,
                  h_ref, lng_ref, lnb_ref, w_in_ref, w_gz_ref,
                  s5_a_ref, w_bu_ref, w_c_ref, s5_d_ref, w_glu_ref,
                  w_up_ref, b_gate_ref, norm_g_ref,
                  conv_w_ref, conv_b_ref, w_r_ref, b_r_ref, w_i_ref, b_i_ref, sp_ref,
                  w_out_ref, ln1g_ref, ln1b_ref,
                  out_ref,
                  s5_state_ref, gla_state_ref, lru_state_ref, xs_ref,
                  bu_ref, hs_ref, slab_ref, oslab_ref, la_ref, lb_ref, xslab_ref=None):
    step = pl.program_id(0)

    @pl.when(step == 0)
    def _():
        s5_state_ref[...] = jnp.zeros_like(s5_state_ref)
        gla_state_ref[...] = jnp.zeros_like(gla_state_ref)
        lru_state_ref[...] = jnp.zeros_like(lru_state_ref)
        xs_ref[0:HIST, :] = jnp.zeros((HIST, LRU_WIDTH), F32)

    if first:
        for b in range(BATCH):
            for j in range(D_MODEL // LANES):
                xslab_ref[j, pl.ds(b, TT, stride=BATCH), :] = (
                    h_ref[b, :, j * LANES:(j + 1) * LANES])
        h = jnp.concatenate([xslab_ref[j] for j in range(D_MODEL // LANES)], axis=1)
        h = _layer_norm(h, lng_ref[...], lnb_ref[...])
    else:
        h = h_ref[...].reshape(ROWS, D_MODEL)
    hb = h.astype(BF16)

    u = _dot(hb, w_in_ref[:, OFF_S5:OFF_S5 + S5_WIDTH])
    q = _dot(hb, w_in_ref[:, OFF_Q:OFF_Q + GLA_KEY_WIDTH]) * (GLA_DK ** -0.5)
    k = _dot(hb, w_in_ref[:, OFF_K:OFF_K + GLA_KEY_WIDTH])
    bu_ref[...] = _dot(u.astype(BF16), w_bu_ref[...])
    v = _dot(hb, w_in_ref[:, OFF_V:OFF_V + GLA_WIDTH])
    gz = _dot(hb, w_gz_ref[...])
    xl = _dot(hb, w_in_ref[:, OFF_LX:OFF_LX + LRU_WIDTH])

    for c in range(S5_NSTATE // S5_LANE_CHUNK):
        lo = c * S5_LANE_CHUNK
        re_sl = slice(lo, lo + S5_LANE_CHUNK)
        im_sl = slice(S5_NSTATE + lo, S5_NSTATE + lo + S5_LANE_CHUNK)
        a_re = jnp.broadcast_to(s5_a_ref[0:1, re_sl], (BATCH, S5_LANE_CHUNK))
        a_im = jnp.broadcast_to(s5_a_ref[0:1, im_sl], (BATCH, S5_LANE_CHUNK))
        h_re = s5_state_ref[:, re_sl]
        h_im = s5_state_ref[:, im_sl]
        for t in range(0, TT, 2):
            pair_re, pair_im = [], []
            for tt in (t, t + 1):
                rows = slice(tt * BATCH, (tt + 1) * BATCH)
                n_re = a_re * h_re - a_im * h_im + bu_ref[rows, re_sl]
                n_im = a_re * h_im + a_im * h_re + bu_ref[rows, im_sl]
                h_re, h_im = n_re, n_im
                pair_re.append(n_re)
                pair_im.append(n_im)
            rows2 = slice(t * BATCH, (t + 2) * BATCH)
            hs_ref[rows2, re_sl] = jnp.concatenate(pair_re, axis=0).astype(BF16)
            hs_ref[rows2, im_sl] = jnp.concatenate(pair_im, axis=0).astype(BF16)
        s5_state_ref[:, re_sl] = h_re
        s5_state_ref[:, im_sl] = h_im

    g_pre = _dot(gz.astype(BF16), w_up_ref[...]) + b_gate_ref[...]
    gb = _dot(hb, w_in_ref[:, OFF_LG:OFF_LG + LRU_WIDTH])
    r = _dot(hb, w_in_ref[:, OFF_R:OFF_R + GLA_WIDTH])

    xs_ref[HIST:HIST + ROWS, :] = xl
    xc = conv_b_ref[...] + conv_w_ref[LRU_CONV - 1:LRU_CONV, :] * xl
    for kk in range(LRU_CONV - 1):
        xc = xc + conv_w_ref[kk:kk + 1, :] * xs_ref[kk * BATCH:kk * BATCH + ROWS, :]
    xs_ref[0:HIST, :] = xl[ROWS - HIST:ROWS, :]
    xcb = xc.astype(BF16)
    gate_r = jax.nn.sigmoid(_dot(xcb, w_r_ref[...]) + b_r_ref[...])
    gate_i = jax.nn.sigmoid(_dot(xcb, w_i_ref[...]) + b_i_ref[...])
    log_a = -LRU_C * gate_r * sp_ref[...]
    la_ref[...] = jnp.exp(log_a)
    th = jnp.tanh(log_a)
    lb_ref[...] = jnp.sqrt(-2.0 * th / (1.0 - th)) * (gate_i * xc)

    g = _log_sigmoid(g_pre) * (1.0 / GLA_TAU)
    bcum = g
    sh = 1
    while sh < TT:
        bcum = bcum + _shift_rows(bcum, sh * BATCH)
        sh *= 2
    blast = bcum[ROWS - BATCH:ROWS, :]
    blast_rows = jnp.broadcast_to(blast[None], (TT, BATCH, GLA_KEY_WIDTH)).reshape(
        ROWS, GLA_KEY_WIDTH)
    qe = q * jnp.exp(bcum)
    ke = k * jnp.exp(-bcum)
    kd = k * jnp.exp(blast_rows - bcum)
    dec = jnp.exp(blast)
    for j in range(GLA_KEY_WIDTH // LANES):
        ls = slice(j * LANES, (j + 1) * LANES)
        slab_ref[SLAB_QE + j] = qe[:, ls]
        slab_ref[SLAB_KE + j] = ke[:, ls]
        slab_ref[SLAB_KD + j] = kd[:, ls]
    for j in range(GLA_WIDTH // LANES):
        slab_ref[SLAB_V + j] = v[:, j * LANES:(j + 1) * LANES]

    y = jnp.concatenate(
        [_dot(hs_ref[0:ROWS // 2, :], w_c_ref[...]), _dot(hs_ref[ROWS // 2:, :], w_c_ref[...])],
        axis=0) + s5_d_ref[...] * u
    y = jax.nn.gelu(y)
    y_s5 = y * jax.nn.sigmoid(_dot(y.astype(BF16), w_glu_ref[...]))
    mix = _dot(y_s5.astype(BF16), w_out_ref[0:S5_WIDTH, :])

    hl = lru_state_ref[...]
    for t in range(TT):
        rows = slice(t * BATCH, (t + 1) * BATCH)
        hl = la_ref[rows, :] * hl + lb_ref[rows, :]
        lb_ref[rows, :] = hl
    lru_state_ref[...] = hl
    y_lru = lb_ref[...] * jax.nn.gelu(gb)
    mix = mix + _dot(y_lru.astype(BF16), w_out_ref[S5_WIDTH + GLA_WIDTH:, :])


    hc = GLA_HEADS * GLA_CHUNK
    row = lax.broadcasted_iota(jnp.int32, (hc, hc), 0)
    col = lax.broadcasted_iota(jnp.int32, (hc, hc), 1)
    head_mask = (row // GLA_CHUNK) == (col // GLA_DK)
    causal_mask = ((row // GLA_CHUNK) == (col // GLA_CHUNK)) & (
        (row % GLA_CHUNK) >= (col % GLA_CHUNK))
    norm_g = norm_g_ref[...]

    def gather(slab0, n, b):
        return jnp.concatenate(
            [slab_ref[slab0 + j, pl.ds(b, TT, stride=BATCH), :] for j in range(n)], axis=1)

    for b in range(BATCH):
        qe_b = gather(SLAB_QE, GLA_KEY_WIDTH // LANES, b)
        ke_b = gather(SLAB_KE, GLA_KEY_WIDTH // LANES, b)
        kd_b = gather(SLAB_KD, GLA_KEY_WIDTH // LANES, b)
        v_b = gather(SLAB_V, GLA_WIDTH // LANES, b)
        q_blk = jnp.where(head_mask, jnp.concatenate([qe_b] * GLA_HEADS, axis=0),
                          0.0).astype(BF16)
        k_blk = jnp.where(head_mask, jnp.concatenate([ke_b] * GLA_HEADS, axis=0),
                          0.0).astype(BF16)
        scores = lax.dot_general(q_blk, k_blk, (((1,), (1,)), ((), ())),
                                 preferred_element_type=F32)
        p = jnp.where(causal_mask, scores, 0.0).astype(BF16)
        v_stack = jnp.concatenate(
            [v_b[:, hh * GLA_DV:(hh + 1) * GLA_DV] for hh in range(GLA_HEADS)],
            axis=0).astype(BF16)
        s_b = gla_state_ref[b]
        o = _dot(p, v_stack) + _dot(q_blk, s_b.astype(BF16))
        o = o * lax.rsqrt(jnp.mean(o * o, axis=-1, keepdims=True) + RMS_EPS) * norm_g
        for hh in range(GLA_HEADS):
            oslab_ref[hh, pl.ds(b, TT, stride=BATCH), :] = (
                o[hh * GLA_CHUNK:(hh + 1) * GLA_CHUNK, :])
        x = jnp.concatenate(
            [kd_b, jnp.broadcast_to(dec[b:b + 1, :], (SUBLANES, GLA_KEY_WIDTH)),
             jnp.zeros((LANES - GLA_CHUNK - SUBLANES, GLA_KEY_WIDTH), F32)], axis=0)
        xt = x.T
        kd_t = xt[:, 0:GLA_CHUNK].astype(BF16)
        dec_col = xt[:, GLA_CHUNK:GLA_CHUNK + 1]
        m = _dot(kd_t, v_b.astype(BF16))
        upd = jnp.concatenate(
            [m[hh * GLA_DK:(hh + 1) * GLA_DK, hh * GLA_DV:(hh + 1) * GLA_DV]
             for hh in range(GLA_HEADS)], axis=0)
        gla_state_ref[b] = dec_col * s_b + upd

    o_all = jnp.concatenate([oslab_ref[j] for j in range(GLA_WIDTH // LANES)], axis=1)
    y_gla = o_all * (r * jax.nn.sigmoid(r))
    mix = mix + _dot(y_gla.astype(BF16), w_out_ref[S5_WIDTH:S5_WIDTH + GLA_WIDTH, :])

    h1 = _layer_norm(DEEPNORM_ALPHA * h + mix, ln1g_ref[...], ln1b_ref[...])
    out_ref[...] = h1.reshape(TT, BATCH, D_MODEL)


def _mlp_kernel(last, h_ref, w1_ref, w2_ref, g_ref, b_ref, out_ref, yslab_ref=None):
    h = h_ref[...]
    hb = h.astype(BF16)
    acc = jnp.zeros((MLP_ROWS, D_MODEL), F32)
    for c in range(D_FF // MLP_FF_CHUNK):
        cs = slice(c * MLP_FF_CHUNK, (c + 1) * MLP_FF_CHUNK)
        a = jnp.maximum(_dot(hb, w1_ref[:, cs]), 0.0)
        acc = acc + _dot((a * a).astype(BF16), w2_ref[cs, :])
    y = _layer_norm(DEEPNORM_ALPHA * h + acc, g_ref[...], b_ref[...])
    if last:
        for j in range(D_MODEL // LANES):
            yslab_ref[j] = y[:, j * LANES:(j + 1) * LANES]
        for b in range(BATCH):
            for j in range(D_MODEL // LANES):
                out_ref[b, :, j * LANES:(j + 1) * LANES] = (
                    yslab_ref[j, pl.ds(b, MLP_TT, stride=BATCH), :])
    else:
        out_ref[...] = y


def _const_spec(shape):
    nd = len(shape)
    return pl.BlockSpec(shape, lambda i, _nd=nd: (0,) * _nd)


def _mixer_call(first, h, weights):
    if first:
        in_specs = [pl.BlockSpec((BATCH, TT, D_MODEL), lambda i: (0, i, 0))]
        extra_scratch = [pltpu.VMEM((D_MODEL // LANES, ROWS, LANES), F32)]
    else:
        in_specs = [pl.BlockSpec((TT, BATCH, D_MODEL), lambda i: (i, 0, 0))]
        extra_scratch = []
    in_specs += [_const_spec(w.shape) for w in weights]
    return pl.pallas_call(
        functools.partial(_mixer_kernel, first),
        grid=(SEQ // TT,),
        in_specs=in_specs,
        out_specs=pl.BlockSpec((TT, BATCH, D_MODEL), lambda i: (i, 0, 0)),
        out_shape=jax.ShapeDtypeStruct((SEQ, BATCH, D_MODEL), F32),
        scratch_shapes=[
            pltpu.VMEM((BATCH, 2 * S5_NSTATE), F32),
            pltpu.VMEM((BATCH, GLA_HEADS * GLA_DK, GLA_DV), F32),
            pltpu.VMEM((BATCH, LRU_WIDTH), F32),
            pltpu.VMEM((HIST + ROWS, LRU_WIDTH), F32),
            pltpu.VMEM((ROWS, 2 * S5_NSTATE), F32),
            pltpu.VMEM((ROWS, 2 * S5_NSTATE), BF16),
            pltpu.VMEM((N_SLABS, ROWS, LANES), F32),
            pltpu.VMEM((GLA_WIDTH // LANES, ROWS, LANES), F32),
            pltpu.VMEM((ROWS, LRU_WIDTH), F32),
            pltpu.VMEM((ROWS, LRU_WIDTH), F32),
        ] + extra_scratch,
        compiler_params=pltpu.CompilerParams(
            dimension_semantics=("arbitrary",), vmem_limit_bytes=VMEM_LIMIT),
        name="mixer",
    )(h, *weights)


def _mlp_call(last, h2d, w1, w2, g, b):
    n = h2d.shape[0]
    if last:
        out_specs = pl.BlockSpec((BATCH, MLP_TT, D_MODEL), lambda i: (0, i, 0))
        out_shape = jax.ShapeDtypeStruct((BATCH, SEQ, D_MODEL), F32)
        scratch = [pltpu.VMEM((D_MODEL // LANES, MLP_ROWS, LANES), F32)]
    else:
        out_specs = pl.BlockSpec((MLP_ROWS, D_MODEL), lambda i: (i, 0))
        out_shape = jax.ShapeDtypeStruct((n, D_MODEL), F32)
        scratch = []
    return pl.pallas_call(
        functools.partial(_mlp_kernel, last),
        grid=(n // MLP_ROWS,),
        in_specs=[pl.BlockSpec((MLP_ROWS, D_MODEL), lambda i: (i, 0)),
                  _const_spec(w1.shape), _const_spec(w2.shape),
                  _const_spec(g.shape), _const_spec(b.shape)],
        out_specs=out_specs,
        out_shape=out_shape,
        scratch_shapes=scratch,
        compiler_params=pltpu.CompilerParams(
            dimension_semantics=("arbitrary",), vmem_limit_bytes=VMEM_LIMIT),
        name="mlp",
    )(h2d, w1, w2, g, b)


def _block_diag(w):
    n, a, b = w.shape
    eye = jnp.eye(n, dtype=w.dtype)
    return jnp.einsum('gab,gh->gahb', w, eye).reshape(n * a, n * b)


def _s5_params(lam_re, lam_im, log_dt, b_re, b_im, c_re, c_im):
    dt = jnp.exp(log_dt)[:, None]
    lr = jnp.minimum(lam_re, -S5_MIN_DECAY)
    li = lam_im
    mag = jnp.exp(lr * dt)
    a_re = mag * jnp.cos(li * dt)
    a_im = mag * jnp.sin(li * dt)
    den = lr * lr + li * li
    f_re = ((a_re - 1.0) * lr + a_im * li) / den
    f_im = (a_im * lr - (a_re - 1.0) * li) / den
    bb_re = f_re[..., None] * b_re - f_im[..., None] * b_im
    bb_im = f_re[..., None] * b_im + f_im[..., None] * b_re
    a = jnp.concatenate([a_re.reshape(1, -1), a_im.reshape(1, -1)], axis=1)
    w_bu = jnp.concatenate([_block_diag(bb_re.transpose(0, 2, 1)),
                            _block_diag(bb_im.transpose(0, 2, 1))], axis=1)
    w_c = jnp.concatenate([_block_diag(c_re.transpose(0, 2, 1)),
                           -_block_diag(c_im.transpose(0, 2, 1))], axis=0)
    return a, w_bu.astype(BF16), w_c.astype(BF16)


def kernel(x, ln_in_g, ln_in_b, w_in, s5_lambda_re, s5_lambda_im, s5_log_dt, s5_b_re, s5_b_im, s5_c_re, s5_c_im, s5_d, s5_w_glu, gla_w_gate_up, gla_b_gate, gla_norm_g, lru_conv_w, lru_conv_b, lru_w_r, lru_b_r, lru_w_i, lru_b_i, lru_lambda, w_out, ln1_g, ln1_b, mlp_w1, mlp_w2, ln2_g, ln2_b):
    assert x.shape == (BATCH, SEQ, D_MODEL)
    row = lambda t: t.reshape(1, -1).astype(F32)
    gz0 = OFF_R + GLA_WIDTH
    h = x
    for l in range(DEPTH):
        w_main = jnp.concatenate(
            [w_in[l][:, :gz0], w_in[l][:, gz0 + GLA_GATE_RANK:]], axis=1).astype(BF16)
        w_gz = jnp.pad(w_in[l][:, gz0:gz0 + GLA_GATE_RANK],
                       ((0, 0), (0, GZ_PAD - GLA_GATE_RANK))).astype(BF16)
        w_up = jnp.pad(gla_w_gate_up[l], ((0, GZ_PAD - GLA_GATE_RANK), (0, 0))).astype(BF16)
        s5_a, w_bu, w_c = _s5_params(s5_lambda_re[l], s5_lambda_im[l], s5_log_dt[l],
                                     s5_b_re[l], s5_b_im[l], s5_c_re[l], s5_c_im[l])
        weights = [
            row(ln_in_g), row(ln_in_b), w_main, w_gz,
            s5_a, w_bu, w_c, row(s5_d[l]), s5_w_glu[l].astype(BF16),
            w_up, row(gla_b_gate[l]), row(gla_norm_g[l]),
            lru_conv_w[l].astype(F32), row(lru_conv_b[l]),
            _block_diag(lru_w_r[l]).astype(BF16), row(lru_b_r[l]),
            _block_diag(lru_w_i[l]).astype(BF16), row(lru_b_i[l]),
            row(jax.nn.softplus(-lru_lambda[l])),
            w_out[l].astype(BF16), row(ln1_g[l]), row(ln1_b[l]),
        ]
        h = _mixer_call(l == 0, h, weights)
        last = l == DEPTH - 1
        h = _mlp_call(last, h.reshape(SEQ * BATCH, D_MODEL), mlp_w1[l].astype(BF16),
                      mlp_w2[l].astype(BF16), row(ln2_g[l]), row(ln2_b[l]))
        if not last:
            h = h.reshape(SEQ, BATCH, D_MODEL)
    return h
```

```python
import functools
import math

import jax
import jax.numpy as jnp
from jax import lax
from jax.experimental import pallas as pl
from jax.experimental.pallas import tpu as pltpu

F32 = jnp.float32
BF16 = jnp.bfloat16

D_MODEL = 1024
BATCH = 8
SEQ = 8192
DEPTH = 2
S5_WIDTH = 256
S5_GROUP = 16
S5_GROUPS = 16
S5_STATE = 64
S5_MIN_DECAY = 1e-4
S5_NSTATE = S5_GROUPS * S5_STATE
GLA_WIDTH = 512
GLA_HEADS = 4
GLA_DV = 128
GLA_DK = 64
GLA_KEY_WIDTH = 256
GLA_GATE_RANK = 16
GLA_TAU = 16.0
GLA_CHUNK = 64
LRU_WIDTH = 256
LRU_BLOCKS = 8
LRU_BLOCK = 32
LRU_CONV = 4
LRU_C = 8.0
D_FF = 4 * D_MODEL
DEEPNORM_ALPHA = (2 * DEPTH) ** 0.25
LN_EPS = 1e-5
RMS_EPS = 1e-6

LANES = 128
SUBLANES = 8
TT = GLA_CHUNK
ROWS = TT * BATCH
HIST = (LRU_CONV - 1) * BATCH
GZ_PAD = LANES
OFF_S5 = 0
OFF_Q = OFF_S5 + S5_WIDTH
OFF_K = OFF_Q + GLA_KEY_WIDTH
OFF_V = OFF_K + GLA_KEY_WIDTH
OFF_R = OFF_V + GLA_WIDTH
OFF_LX = OFF_R + GLA_WIDTH
OFF_LG = OFF_LX + LRU_WIDTH
D_IN_MAIN = OFF_LG + LRU_WIDTH
SLAB_QE = 0
SLAB_KE = SLAB_QE + GLA_KEY_WIDTH // LANES
SLAB_KD = SLAB_KE + GLA_KEY_WIDTH // LANES
SLAB_V = SLAB_KD + GLA_KEY_WIDTH // LANES
N_SLABS = SLAB_V + GLA_WIDTH // LANES
S5_LANE_CHUNK = 512
MLP_ROWS = 1024
MLP_TT = MLP_ROWS // BATCH
MLP_ROW_BLOCK = 512
MLP_FF_CHUNK = 1024
VMEM_LIMIT = 56 * 1024 * 1024


def _layer_norm(x, g, b):
    mu = jnp.mean(x, axis=-1, keepdims=True)
    xc = x - mu
    var = jnp.mean(xc * xc, axis=-1, keepdims=True)
    return xc * lax.rsqrt(var + LN_EPS) * g + b


_dot = functools.partial(jnp.dot, preferred_element_type=F32)


def _log_sigmoid(x):
    return jnp.minimum(x, 0.0) - jnp.log(1.0 + jnp.exp(-jnp.abs(x)))


def _shift_rows(x, n):
    return jnp.concatenate([jnp.zeros((n, x.shape[1]), x.dtype), x[:-n]], axis=0)


def _mixer_kernel(first,
                  h_ref, lng_ref, lnb_ref, w_in_ref, w_gz_ref,
                  s5_a_ref, w_bu_ref, w_c_ref, s5_d_ref, w_glu_ref,
                  w_up_ref, b_gate_ref, norm_g_ref,
                  conv_w_ref, conv_b_ref, w_r_ref, b_r_ref, w_i_ref, b_i_ref, sp_ref,
                  w_out_ref, ln1g_ref, ln1b_ref,
                  out_ref,
                  s5_state_ref, gla_state_ref, lru_state_ref, xs_ref,
                  bu_ref, hs_ref, slab_ref, oslab_ref, la_ref, lb_ref,
                  resid_ref, silu_ref, xslab_ref=None):
    step = pl.program_id(0)

    @pl.when(step == 0)
    def _():
        s5_state_ref[...] = jnp.zeros_like(s5_state_ref)
        gla_state_ref[...] = jnp.zeros_like(gla_state_ref)
        lru_state_ref[...] = jnp.zeros_like(lru_state_ref)
        xs_ref[0:HIST, :] = jnp.zeros((HIST, LRU_WIDTH), F32)
        oslab_ref[...] = jnp.zeros_like(oslab_ref)
        resid_ref[...] = jnp.zeros_like(resid_ref)
        silu_ref[...] = jnp.zeros_like(silu_ref)

    for hf in range(2):
        rs = slice(hf * (ROWS // 2), (hf + 1) * (ROWS // 2))
        o_hf = jnp.concatenate(
            [oslab_ref[j, rs, :] for j in range(GLA_WIDTH // LANES)], axis=1)
        y_gla = o_hf * silu_ref[rs, :]
        pre = resid_ref[rs, :] + _dot(y_gla.astype(BF16),
                                      w_out_ref[S5_WIDTH:S5_WIDTH + GLA_WIDTH, :])
        h1 = _layer_norm(pre, ln1g_ref[...], ln1b_ref[...])
        out_ref[hf * (TT // 2):(hf + 1) * (TT // 2)] = h1.reshape(TT // 2, BATCH, D_MODEL)

    if first:
        for b in range(BATCH):
            for j in range(D_MODEL // LANES):
                xslab_ref[j, pl.ds(b, TT, stride=BATCH), :] = (
                    h_ref[b, :, j * LANES:(j + 1) * LANES])
        h = jnp.concatenate([xslab_ref[j] for j in range(D_MODEL // LANES)], axis=1)
        h = _layer_norm(h, lng_ref[...], lnb_ref[...])
    else:
        h = h_ref[...].reshape(ROWS, D_MODEL)
    hb = h.astype(BF16)

    u = _dot(hb, w_in_ref[:, OFF_S5:OFF_S5 + S5_WIDTH])
    q = _dot(hb, w_in_ref[:, OFF_Q:OFF_Q + GLA_KEY_WIDTH]) * (GLA_DK ** -0.5)
    k = _dot(hb, w_in_ref[:, OFF_K:OFF_K + GLA_KEY_WIDTH])
    bu_ref[...] = _dot(u.astype(BF16), w_bu_ref[...])
    v = _dot(hb, w_in_ref[:, OFF_V:OFF_V + GLA_WIDTH])
    gz = _dot(hb, w_gz_ref[...])
    xl = _dot(hb, w_in_ref[:, OFF_LX:OFF_LX + LRU_WIDTH])

    for c in range(S5_NSTATE // S5_LANE_CHUNK):
        lo = c * S5_LANE_CHUNK
        re_sl = slice(lo, lo + S5_LANE_CHUNK)
        im_sl = slice(S5_NSTATE + lo, S5_NSTATE + lo + S5_LANE_CHUNK)
        a_re = jnp.broadcast_to(s5_a_ref[0:1, re_sl], (BATCH, S5_LANE_CHUNK))
        a_im = jnp.broadcast_to(s5_a_ref[0:1, im_sl], (BATCH, S5_LANE_CHUNK))
        h_re = s5_state_ref[:, re_sl]
        h_im = s5_state_ref[:, im_sl]
        for t in range(0, TT, 2):
            pair_re, pair_im = [], []
            for tt in (t, t + 1):
                rows = slice(tt * BATCH, (tt + 1) * BATCH)
                n_re = a_re * h_re - a_im * h_im + bu_ref[rows, re_sl]
                n_im = a_re * h_im + a_im * h_re + bu_ref[rows, im_sl]
                h_re, h_im = n_re, n_im
                pair_re.append(n_re)
                pair_im.append(n_im)
            rows2 = slice(t * BATCH, (t + 2) * BATCH)
            hs_ref[rows2, re_sl] = jnp.concatenate(pair_re, axis=0).astype(BF16)
            hs_ref[rows2, im_sl] = jnp.concatenate(pair_im, axis=0).astype(BF16)
        s5_state_ref[:, re_sl] = h_re
        s5_state_ref[:, im_sl] = h_im

    g_pre = _dot(gz.astype(BF16), w_up_ref[...]) + b_gate_ref[...]
    gb = _dot(hb, w_in_ref[:, OFF_LG:OFF_LG + LRU_WIDTH])
    r = _dot(hb, w_in_ref[:, OFF_R:OFF_R + GLA_WIDTH])
    silu_ref[...] = r * jax.nn.sigmoid(r)

    xs_ref[HIST:HIST + ROWS, :] = xl
    xc = conv_b_ref[...] + conv_w_ref[LRU_CONV - 1:LRU_CONV, :] * xl
    for kk in range(LRU_CONV - 1):
        xc = xc + conv_w_ref[kk:kk + 1, :] * xs_ref[kk * BATCH:kk * BATCH + ROWS, :]
    xs_ref[0:HIST, :] = xl[ROWS - HIST:ROWS, :]
    xcb = xc.astype(BF16)
    gate_r = jax.nn.sigmoid(_dot(xcb, w_r_ref[...]) + b_r_ref[...])
    gate_i = jax.nn.sigmoid(_dot(xcb, w_i_ref[...]) + b_i_ref[...])
    log_a = -LRU_C * gate_r * sp_ref[...]
    a_lru = jnp.exp(log_a)
    la_ref[...] = a_lru
    lb_ref[...] = jnp.sqrt(1.0 - a_lru * a_lru) * (gate_i * xc)

    g = _log_sigmoid(g_pre) * (1.0 / GLA_TAU)
    bcum = g
    sh = 1
    while sh < TT:
        bcum = bcum + _shift_rows(bcum, sh * BATCH)
        sh *= 2
    blast = bcum[ROWS - BATCH:ROWS, :]
    blast_rows = jnp.broadcast_to(blast[None], (TT, BATCH, GLA_KEY_WIDTH)).reshape(
        ROWS, GLA_KEY_WIDTH)
    qe = q * jnp.exp(bcum)
    ke = k * jnp.exp(-bcum)
    kd = k * jnp.exp(blast_rows - bcum)
    dec = jnp.exp(blast)
    for j in range(GLA_KEY_WIDTH // LANES):
        ls = slice(j * LANES, (j + 1) * LANES)
        slab_ref[SLAB_QE + j] = qe[:, ls]
        slab_ref[SLAB_KE + j] = ke[:, ls]
        slab_ref[SLAB_KD + j] = kd[:, ls]
    for j in range(GLA_WIDTH // LANES):
        slab_ref[SLAB_V + j] = v[:, j * LANES:(j + 1) * LANES]

    y = jnp.concatenate(
        [_dot(hs_ref[0:ROWS // 2, :], w_c_ref[...]), _dot(hs_ref[ROWS // 2:, :], w_c_ref[...])],
        axis=0) + s5_d_ref[...] * u
    y = jax.nn.gelu(y)
    y_s5 = y * jax.nn.sigmoid(_dot(y.astype(BF16), w_glu_ref[...]))
    mix = _dot(y_s5.astype(BF16), w_out_ref[0:S5_WIDTH, :])

    hl = lru_state_ref[...]
    for t in range(TT):
        rows = slice(t * BATCH, (t + 1) * BATCH)
        hl = la_ref[rows, :] * hl + lb_ref[rows, :]
        lb_ref[rows, :] = hl
    lru_state_ref[...] = hl
    y_lru = lb_ref[...] * jax.nn.gelu(gb)
    mix = mix + _dot(y_lru.astype(BF16), w_out_ref[S5_WIDTH + GLA_WIDTH:, :])
    resid_ref[...] = DEEPNORM_ALPHA * h + mix


    hc = GLA_HEADS * GLA_CHUNK
    row = lax.broadcasted_iota(jnp.int32, (hc, hc), 0)
    col = lax.broadcasted_iota(jnp.int32, (hc, hc), 1)
    head_mask = (row // GLA_CHUNK) == (col // GLA_DK)
    causal_mask = ((row // GLA_CHUNK) == (col // GLA_CHUNK)) & (
        (row % GLA_CHUNK) >= (col % GLA_CHUNK))
    norm_g = norm_g_ref[...]

    def gather(slab0, n, b):
        return jnp.concatenate(
            [slab_ref[slab0 + j, pl.ds(b, TT, stride=BATCH), :] for j in range(n)], axis=1)

    q_blks, ps, v_bbs, kd_ts, dec_cols = [], [], [], [], []
    for b in range(BATCH):
        qe_b = gather(SLAB_QE, GLA_KEY_WIDTH // LANES, b)
        ke_b = gather(SLAB_KE, GLA_KEY_WIDTH // LANES, b)
        q_blk = jnp.where(head_mask, jnp.concatenate([qe_b] * GLA_HEADS, axis=0),
                          0.0).astype(BF16)
        k_blk = jnp.where(head_mask, jnp.concatenate([ke_b] * GLA_HEADS, axis=0),
                          0.0).astype(BF16)
        scores = lax.dot_general(q_blk, k_blk, (((1,), (1,)), ((), ())),
                                 preferred_element_type=F32)
        q_blks.append(q_blk)
        ps.append(jnp.where(causal_mask, scores, 0.0).astype(BF16))
    for b in range(BATCH):
        kd_b = gather(SLAB_KD, GLA_KEY_WIDTH // LANES, b)
        x = jnp.concatenate(
            [kd_b, jnp.broadcast_to(dec[b:b + 1, :], (SUBLANES, GLA_KEY_WIDTH)),
             jnp.zeros((LANES - GLA_CHUNK - SUBLANES, GLA_KEY_WIDTH), F32)], axis=0)
        xt = x.T
        kd_ts.append(xt[:, 0:GLA_CHUNK].astype(BF16))
        dec_cols.append(xt[:, GLA_CHUNK:GLA_CHUNK + 1])
    for b in range(BATCH):
        v_bb = gather(SLAB_V, GLA_WIDTH // LANES, b).astype(BF16)
        v_bbs.append(v_bb)
        v_stack = jnp.concatenate(
            [v_bb[:, hh * GLA_DV:(hh + 1) * GLA_DV] for hh in range(GLA_HEADS)],
            axis=0)
        s_b = gla_state_ref[b]
        o = _dot(ps[b], v_stack) + _dot(q_blks[b], s_b.astype(BF16))
        o = o * lax.rsqrt(jnp.mean(o * o, axis=-1, keepdims=True) + RMS_EPS) * norm_g
        for hh in range(GLA_HEADS):
            oslab_ref[hh, pl.ds(b, TT, stride=BATCH), :] = (
                o[hh * GLA_CHUNK:(hh + 1) * GLA_CHUNK, :])
    for b in range(BATCH):
        upd = jnp.concatenate(
            [_dot(kd_ts[b][hh * GLA_DK:(hh + 1) * GLA_DK, :],
                  v_bbs[b][:, hh * GLA_DV:(hh + 1) * GLA_DV])
             for hh in range(GLA_HEADS)], axis=0)
        gla_state_ref[b] = dec_cols[b] * gla_state_ref[b] + upd


def _mlp_kernel(last, h_ref, w1_ref, w2_ref, g_ref, b_ref, out_ref, yslab_ref=None):
    ys = []
    for rb in range(MLP_ROWS // MLP_ROW_BLOCK):
        h = h_ref[rb * MLP_ROW_BLOCK:(rb + 1) * MLP_ROW_BLOCK, :]
        hb = h.astype(BF16)
        hid = []
        for c in range(D_FF // MLP_FF_CHUNK):
            cs = slice(c * MLP_FF_CHUNK, (c + 1) * MLP_FF_CHUNK)
            a = jnp.maximum(_dot(hb, w1_ref[:, cs]), 0.0)
            hid.append((a * a).astype(BF16))
        ff = _dot(jnp.concatenate(hid, axis=1), w2_ref[...])
        ys.append(_layer_norm(DEEPNORM_ALPHA * h + ff, g_ref[...], b_ref[...]))
    y = jnp.concatenate(ys, axis=0)
    if last:
        for j in range(D_MODEL // LANES):
            yslab_ref[j] = y[:, j * LANES:(j + 1) * LANES]
        for b in range(BATCH):
            for j in range(D_MODEL // LANES):
                out_ref[b, :, j * LANES:(j + 1) * LANES] = (
                    yslab_ref[j, pl.ds(b, MLP_TT, stride=BATCH), :])
    else:
        out_ref[...] = y


def _const_spec(shape):
    nd = len(shape)
    return pl.BlockSpec(shape, lambda i, _nd=nd: (0,) * _nd)


def _mixer_call(first, h, weights):
    n_tiles = SEQ // TT
    tile_in = lambda i: jnp.minimum(i, n_tiles - 1)
    tile_out = lambda i: jnp.maximum(i - 1, 0)
    if first:
        in_specs = [pl.BlockSpec((BATCH, TT, D_MODEL), lambda i: (0, tile_in(i), 0))]
        extra_scratch = [pltpu.VMEM((D_MODEL // LANES, ROWS, LANES), F32)]
    else:
        in_specs = [pl.BlockSpec((TT, BATCH, D_MODEL), lambda i: (tile_in(i), 0, 0))]
        extra_scratch = []
    in_specs += [_const_spec(w.shape) for w in weights]
    return pl.pallas_call(
        functools.partial(_mixer_kernel, first),
        grid=(n_tiles + 1,),
        in_specs=in_specs,
        out_specs=pl.BlockSpec((TT, BATCH, D_MODEL), lambda i: (tile_out(i), 0, 0)),
        out_shape=jax.ShapeDtypeStruct((SEQ, BATCH, D_MODEL), F32),
        scratch_shapes=[
            pltpu.VMEM((BATCH, 2 * S5_NSTATE), F32),
            pltpu.VMEM((BATCH, GLA_HEADS * GLA_DK, GLA_DV), F32),
            pltpu.VMEM((BATCH, LRU_WIDTH), F32),
            pltpu.VMEM((HIST + ROWS, LRU_WIDTH), F32),
            pltpu.VMEM((ROWS, 2 * S5_NSTATE), F32),
            pltpu.VMEM((ROWS, 2 * S5_NSTATE), BF16),
            pltpu.VMEM((N_SLABS, ROWS, LANES), F32),
            pltpu.VMEM((GLA_WIDTH // LANES, ROWS, LANES), F32),
            pltpu.VMEM((ROWS, LRU_WIDTH), F32),
            pltpu.VMEM((ROWS, LRU_WIDTH), F32),
            pltpu.VMEM((ROWS, D_MODEL), F32),
            pltpu.VMEM((ROWS, GLA_WIDTH), F32),
        ] + extra_scratch,
        compiler_params=pltpu.CompilerParams(
            dimension_semantics=("arbitrary",), vmem_limit_bytes=VMEM_LIMIT),
        name="mixer",
    )(h, *weights)


def _mlp_call(last, h2d, w1, w2, g, b):
    n = h2d.shape[0]
    if last:
        out_specs = pl.BlockSpec((BATCH, MLP_TT, D_MODEL), lambda i: (0, i, 0))
        out_shape = jax.ShapeDtypeStruct((BATCH, SEQ, D_MODEL), F32)
        scratch = [pltpu.VMEM((D_MODEL // LANES, MLP_ROWS, LANES), F32)]
    else:
        out_specs = pl.BlockSpec((MLP_ROWS, D_MODEL), lambda i: (i, 0))
        out_shape = jax.ShapeDtypeStruct((n, D_MODEL), F32)
        scratch = []
    return pl.pallas_call(
        functools.partial(_mlp_kernel, last),
        grid=(n // MLP_ROWS,),
        in_specs=[pl.BlockSpec((MLP_ROWS, D_MODEL), lambda i: (i, 0)),
                  _const_spec(w1.shape), _const_spec(w2.shape),
                  _const_spec(g.shape), _const_spec(b.shape)],
        out_specs=out_specs,
        out_shape=out_shape,
        scratch_shapes=scratch,
        compiler_params=pltpu.CompilerParams(
            dimension_semantics=("arbitrary",), vmem_limit_bytes=VMEM_LIMIT),
        name="mlp",
    )(h2d, w1, w2, g, b)


def _block_diag(w):
    n, a, b = w.shape
    eye = jnp.eye(n, dtype=w.dtype)
    return jnp.einsum('gab,gh->gahb', w, eye).reshape(n * a, n * b)


def _s5_params(lam_re, lam_im, log_dt, b_re, b_im, c_re, c_im):
    dt = jnp.exp(log_dt)[:, None]
    lr = jnp.minimum(lam_re, -S5_MIN_DECAY)
    li = lam_im
    mag = jnp.exp(lr * dt)
    a_re = mag * jnp.cos(li * dt)
    a_im = mag * jnp.sin(li * dt)
    den = lr * lr + li * li
    f_re = ((a_re - 1.0) * lr + a_im * li) / den
    f_im = (a_im * lr - (a_re - 1.0) * li) / den
    bb_re = f_re[..., None] * b_re - f_im[..., None] * b_im
    bb_im = f_re[..., None] * b_im + f_im[..., None] * b_re
    a = jnp.concatenate([a_re.reshape(1, -1), a_im.reshape(1, -1)], axis=1)
    w_bu = jnp.concatenate([_block_diag(bb_re.transpose(0, 2, 1)),
                            _block_diag(bb_im.transpose(0, 2, 1))], axis=1)
    w_c = jnp.concatenate([_block_diag(c_re.transpose(0, 2, 1)),
                           -_block_diag(c_im.transpose(0, 2, 1))], axis=0)
    return a, w_bu.astype(BF16), w_c.astype(BF16)


def kernel(x, ln_in_g, ln_in_b, w_in, s5_lambda_re, s5_lambda_im, s5_log_dt, s5_b_re, s5_b_im, s5_c_re, s5_c_im, s5_d, s5_w_glu, gla_w_gate_up, gla_b_gate, gla_norm_g, lru_conv_w, lru_conv_b, lru_w_r, lru_b_r, lru_w_i, lru_b_i, lru_lambda, w_out, ln1_g, ln1_b, mlp_w1, mlp_w2, ln2_g, ln2_b):
    assert x.shape == (BATCH, SEQ, D_MODEL)
    row = lambda t: t.reshape(1, -1).astype(F32)
    gz0 = OFF_R + GLA_WIDTH
    h = x
    for l in range(DEPTH):
        w_main = jnp.concatenate(
            [w_in[l][:, :gz0], w_in[l][:, gz0 + GLA_GATE_RANK:]], axis=1).astype(BF16)
        w_gz = jnp.pad(w_in[l][:, gz0:gz0 + GLA_GATE_RANK],
                       ((0, 0), (0, GZ_PAD - GLA_GATE_RANK))).astype(BF16)
        w_up = jnp.pad(gla_w_gate_up[l], ((0, GZ_PAD - GLA_GATE_RANK), (0, 0))).astype(BF16)
        s5_a, w_bu, w_c = _s5_params(s5_lambda_re[l], s5_lambda_im[l], s5_log_dt[l],
                                     s5_b_re[l], s5_b_im[l], s5_c_re[l], s5_c_im[l])
        weights = [
            row(ln_in_g), row(ln_in_b), w_main, w_gz,
            s5_a, w_bu, w_c, row(s5_d[l]), s5_w_glu[l].astype(BF16),
            w_up, row(gla_b_gate[l]), row(gla_norm_g[l]),
            lru_conv_w[l].astype(F32), row(lru_conv_b[l]),
            _block_diag(lru_w_r[l]).astype(BF16), row(lru_b_r[l]),
            _block_diag(lru_w_i[l]).astype(BF16), row(lru_b_i[l]),
            row(jax.nn.softplus(-lru_lambda[l])),
            w_out[l].astype(BF16), row(ln1_g[l]), row(ln1_b[l]),
        ]
        h = _mixer_call(l == 0, h, weights)
        last = l == DEPTH - 1
        h = _mlp_call(last, h.reshape(SEQ * BATCH, D_MODEL), mlp_w1[l].astype(BF16),
                      mlp_w2[l].astype(BF16), row(ln2_g[l]), row(ln2_b[l]))
        if not last:
            h = h.reshape(SEQ, BATCH, D_MODEL)
    return h
```

```python
import functools
import math

import jax
import jax.numpy as jnp
from jax import lax
from jax.experimental import pallas as pl
from jax.experimental.pallas import tpu as pltpu

F32 = jnp.float32
BF16 = jnp.bfloat16

D_MODEL = 1024
BATCH = 8
SEQ = 8192
DEPTH = 2
S5_WIDTH = 256
S5_GROUP = 16
S5_GROUPS = 16
S5_STATE = 64
S5_MIN_DECAY = 1e-4
S5_NSTATE = S5_GROUPS * S5_STATE
GLA_WIDTH = 512
GLA_HEADS = 4
GLA_DV = 128
GLA_DK = 64
GLA_KEY_WIDTH = 256
GLA_GATE_RANK = 16
GLA_TAU = 16.0
GLA_CHUNK = 64
LRU_WIDTH = 256
LRU_BLOCKS = 8
LRU_BLOCK = 32
LRU_CONV = 4
LRU_C = 8.0
D_FF = 4 * D_MODEL
DEEPNORM_ALPHA = (2 * DEPTH) ** 0.25
LN_EPS = 1e-5
RMS_EPS = 1e-6

LANES = 128
SUBLANES = 8
TT = GLA_CHUNK
ROWS = TT * BATCH
HIST = (LRU_CONV - 1) * BATCH
GZ_PAD = LANES
OFF_S5 = 0
OFF_Q = OFF_S5 + S5_WIDTH
OFF_K = OFF_Q + GLA_KEY_WIDTH
OFF_V = OFF_K + GLA_KEY_WIDTH
OFF_R = OFF_V + GLA_WIDTH
OFF_LX = OFF_R + GLA_WIDTH
OFF_LG = OFF_LX + LRU_WIDTH
D_IN_MAIN = OFF_LG + LRU_WIDTH
SLAB_QE = 0
SLAB_KE = SLAB_QE + GLA_KEY_WIDTH // LANES
SLAB_KD = SLAB_KE + GLA_KEY_WIDTH // LANES
SLAB_V = SLAB_KD + GLA_KEY_WIDTH // LANES
N_SLABS = SLAB_V + GLA_WIDTH // LANES
S5_CHUNK = 2
S5_PARTS = S5_WIDTH // LANES
S5_PART_STATES = S5_NSTATE // S5_PARTS
S5_NCHUNK = TT // S5_CHUNK
S5_CROWS = S5_NCHUNK * BATCH
MLP_ROWS = 1024
MLP_TT = MLP_ROWS // BATCH
MLP_ROW_BLOCK = 512
MLP_FF_CHUNK = 1024
VMEM_LIMIT = 56 * 1024 * 1024


def _layer_norm(x, g, b):
    mu = jnp.mean(x, axis=-1, keepdims=True)
    xc = x - mu
    var = jnp.mean(xc * xc, axis=-1, keepdims=True)
    return xc * lax.rsqrt(var + LN_EPS) * g + b


_dot = functools.partial(jnp.dot, preferred_element_type=F32)


def _log_sigmoid(x):
    return jnp.minimum(x, 0.0) - jnp.log(1.0 + jnp.exp(-jnp.abs(x)))


def _shift_rows(x, n):
    return jnp.concatenate([jnp.zeros((n, x.shape[1]), x.dtype), x[:-n]], axis=0)


def _mixer_kernel(first,
                  h_ref, lng_ref, lnb_ref, w_in_ref, w_gz_ref,
                  s5_a2_ref, w_e_ref, w_y_ref, w_t_ref, s5_d_ref, w_glu_ref,
                  w_up_ref, b_gate_ref, norm_g_ref,
                  conv_w_ref, conv_b_ref, w_r_ref, b_r_ref, w_i_ref, b_i_ref, sp_ref,
                  w_out_ref, ln1g_ref, ln1b_ref,
                  out_ref,
                  s5_state_ref, gla_state_ref, lru_state_ref, xs_ref,
                  e_ref, hs_ref, u4_ref, y4_ref, slab_ref, oslab_ref, la_ref, lb_ref,
                  resid_ref, silu_ref, xslab_ref=None):
    step = pl.program_id(0)

    @pl.when(step == 0)
    def _():
        s5_state_ref[...] = jnp.zeros_like(s5_state_ref)
        gla_state_ref[...] = jnp.zeros_like(gla_state_ref)
        lru_state_ref[...] = jnp.zeros_like(lru_state_ref)
        xs_ref[0:HIST, :] = jnp.zeros((HIST, LRU_WIDTH), F32)
        oslab_ref[...] = jnp.zeros_like(oslab_ref)
        resid_ref[...] = jnp.zeros_like(resid_ref)
        silu_ref[...] = jnp.zeros_like(silu_ref)

    for hf in range(2):
        rs = slice(hf * (ROWS // 2), (hf + 1) * (ROWS // 2))
        o_hf = jnp.concatenate(
            [oslab_ref[j, rs, :] for j in range(GLA_WIDTH // LANES)], axis=1)
        y_gla = o_hf * silu_ref[rs, :]
        pre = resid_ref[rs, :] + _dot(y_gla.astype(BF16),
                                      w_out_ref[S5_WIDTH:S5_WIDTH + GLA_WIDTH, :])
        h1 = _layer_norm(pre, ln1g_ref[...], ln1b_ref[...])
        out_ref[hf * (TT // 2):(hf + 1) * (TT // 2)] = h1.reshape(TT // 2, BATCH, D_MODEL)

    if first:
        for b in range(BATCH):
            for j in range(D_MODEL // LANES):
                xslab_ref[j, pl.ds(b, TT, stride=BATCH), :] = (
                    h_ref[b, :, j * LANES:(j + 1) * LANES])
        h = jnp.concatenate([xslab_ref[j] for j in range(D_MODEL // LANES)], axis=1)
        h = _layer_norm(h, lng_ref[...], lnb_ref[...])
    else:
        h = h_ref[...].reshape(ROWS, D_MODEL)
    hb = h.astype(BF16)

    u = _dot(hb, w_in_ref[:, OFF_S5:OFF_S5 + S5_WIDTH])
    q = _dot(hb, w_in_ref[:, OFF_Q:OFF_Q + GLA_KEY_WIDTH]) * (GLA_DK ** -0.5)
    k = _dot(hb, w_in_ref[:, OFF_K:OFF_K + GLA_KEY_WIDTH])
    u4_ref[...] = u.reshape(S5_NCHUNK, S5_CHUNK, BATCH, S5_WIDTH)
    uc = []
    for n in range(S5_PARTS):
        ls = slice(n * LANES, (n + 1) * LANES)
        uc.append(jnp.concatenate(
            [u4_ref[:, s, :, ls].reshape(S5_CROWS, LANES) for s in range(S5_CHUNK)],
            axis=1).astype(BF16))
        e_ref[:, n * 2 * S5_PART_STATES:(n + 1) * 2 * S5_PART_STATES] = _dot(uc[n], w_e_ref[n])
    v = _dot(hb, w_in_ref[:, OFF_V:OFF_V + GLA_WIDTH])
    gz = _dot(hb, w_gz_ref[...])
    xl = _dot(hb, w_in_ref[:, OFF_LX:OFF_LX + LRU_WIDTH])

    for n in range(S5_PARTS):
        lo = n * 2 * S5_PART_STATES
        re_sl = slice(lo, lo + S5_PART_STATES)
        im_sl = slice(lo + S5_PART_STATES, lo + 2 * S5_PART_STATES)
        a_re = jnp.broadcast_to(s5_a2_ref[0:1, re_sl], (BATCH, S5_PART_STATES))
        a_im = jnp.broadcast_to(s5_a2_ref[0:1, im_sl], (BATCH, S5_PART_STATES))
        h_re = s5_state_ref[:, re_sl]
        h_im = s5_state_ref[:, im_sl]
        for j in range(0, S5_NCHUNK, 2):
            pair_re, pair_im = [], []
            for jj in (j, j + 1):
                rows = slice(jj * BATCH, (jj + 1) * BATCH)
                pair_re.append(h_re)
                pair_im.append(h_im)
                n_re = a_re * h_re - a_im * h_im + e_ref[rows, re_sl]
                n_im = a_re * h_im + a_im * h_re + e_ref[rows, im_sl]
                h_re, h_im = n_re, n_im
            rows2 = slice(j * BATCH, (j + 2) * BATCH)
            hs_ref[rows2, re_sl] = jnp.concatenate(pair_re, axis=0).astype(BF16)
            hs_ref[rows2, im_sl] = jnp.concatenate(pair_im, axis=0).astype(BF16)
        s5_state_ref[:, re_sl] = h_re
        s5_state_ref[:, im_sl] = h_im

    g_pre = _dot(gz.astype(BF16), w_up_ref[...]) + b_gate_ref[...]
    gb = _dot(hb, w_in_ref[:, OFF_LG:OFF_LG + LRU_WIDTH])
    r = _dot(hb, w_in_ref[:, OFF_R:OFF_R + GLA_WIDTH])
    silu_ref[...] = r * jax.nn.sigmoid(r)

    xs_ref[HIST:HIST + ROWS, :] = xl
    xc = conv_b_ref[...] + conv_w_ref[LRU_CONV - 1:LRU_CONV, :] * xl
    for kk in range(LRU_CONV - 1):
        xc = xc + conv_w_ref[kk:kk + 1, :] * xs_ref[kk * BATCH:kk * BATCH + ROWS, :]
    xs_ref[0:HIST, :] = xl[ROWS - HIST:ROWS, :]
    xcb = xc.astype(BF16)
    gate_r = jax.nn.sigmoid(_dot(xcb, w_r_ref[...]) + b_r_ref[...])
    gate_i = jax.nn.sigmoid(_dot(xcb, w_i_ref[...]) + b_i_ref[...])
    log_a = -LRU_C * gate_r * sp_ref[...]
    a_lru = jnp.exp(log_a)
    la_ref[...] = a_lru
    lb_ref[...] = jnp.sqrt(1.0 - a_lru * a_lru) * (gate_i * xc)

    g = _log_sigmoid(g_pre) * (1.0 / GLA_TAU)
    bcum = g
    sh = 1
    while sh < TT:
        bcum = bcum + _shift_rows(bcum, sh * BATCH)
        sh *= 2
    blast = bcum[ROWS - BATCH:ROWS, :]
    blast_rows = jnp.broadcast_to(blast[None], (TT, BATCH, GLA_KEY_WIDTH)).reshape(
        ROWS, GLA_KEY_WIDTH)
    qe = q * jnp.exp(bcum)
    ke = k * jnp.exp(-bcum)
    kd = k * jnp.exp(blast_rows - bcum)
    dec = jnp.exp(blast)
    for j in range(GLA_KEY_WIDTH // LANES):
        ls = slice(j * LANES, (j + 1) * LANES)
        slab_ref[SLAB_QE + j] = qe[:, ls]
        slab_ref[SLAB_KE + j] = ke[:, ls]
        slab_ref[SLAB_KD + j] = kd[:, ls]
    for j in range(GLA_WIDTH // LANES):
        slab_ref[SLAB_V + j] = v[:, j * LANES:(j + 1) * LANES]

    for n in range(S5_PARTS):
        lo = n * 2 * S5_PART_STATES
        yc = (_dot(hs_ref[:, lo:lo + 2 * S5_PART_STATES], w_y_ref[n])
              + _dot(uc[n], w_t_ref[n]))
        for s in range(S5_CHUNK):
            y4_ref[:, s, :, n * LANES:(n + 1) * LANES] = (
                yc[:, s * LANES:(s + 1) * LANES].reshape(S5_NCHUNK, BATCH, LANES))
    y = y4_ref[...].reshape(ROWS, S5_WIDTH) + s5_d_ref[...] * u
    y = jax.nn.gelu(y)
    y_s5 = y * jax.nn.sigmoid(_dot(y.astype(BF16), w_glu_ref[...]))
    mix = _dot(y_s5.astype(BF16), w_out_ref[0:S5_WIDTH, :])

    hl = lru_state_ref[...]
    for t in range(TT):
        rows = slice(t * BATCH, (t + 1) * BATCH)
        hl = la_ref[rows, :] * hl + lb_ref[rows, :]
        lb_ref[rows, :] = hl
    lru_state_ref[...] = hl
    y_lru = lb_ref[...] * jax.nn.gelu(gb)
    mix = mix + _dot(y_lru.astype(BF16), w_out_ref[S5_WIDTH + GLA_WIDTH:, :])
    resid_ref[...] = DEEPNORM_ALPHA * h + mix


    hc = GLA_HEADS * GLA_CHUNK
    row = lax.broadcasted_iota(jnp.int32, (hc, hc), 0)
    col = lax.broadcasted_iota(jnp.int32, (hc, hc), 1)
    head_mask = (row // GLA_CHUNK) == (col // GLA_DK)
    causal_mask = ((row // GLA_CHUNK) == (col // GLA_CHUNK)) & (
        (row % GLA_CHUNK) >= (col % GLA_CHUNK))
    norm_g = norm_g_ref[...]

    def gather(slab0, n, b):
        return jnp.concatenate(
            [slab_ref[slab0 + j, pl.ds(b, TT, stride=BATCH), :] for j in range(n)], axis=1)

    q_blks, ps, v_bbs, kd_ts, dec_cols = [], [], [], [], []
    for b in range(BATCH):
        qe_b = gather(SLAB_QE, GLA_KEY_WIDTH // LANES, b)
        ke_b = gather(SLAB_KE, GLA_KEY_WIDTH // LANES, b)
        q_blk = jnp.where(head_mask, jnp.concatenate([qe_b] * GLA_HEADS, axis=0),
                          0.0).astype(BF16)
        k_blk = jnp.where(head_mask, jnp.concatenate([ke_b] * GLA_HEADS, axis=0),
                          0.0).astype(BF16)
        scores = lax.dot_general(q_blk, k_blk, (((1,), (1,)), ((), ())),
                                 preferred_element_type=F32)
        q_blks.append(q_blk)
        ps.append(jnp.where(causal_mask, scores, 0.0).astype(BF16))
    for b in range(BATCH):
        kd_b = gather(SLAB_KD, GLA_KEY_WIDTH // LANES, b)
        x = jnp.concatenate(
            [kd_b, jnp.broadcast_to(dec[b:b + 1, :], (SUBLANES, GLA_KEY_WIDTH)),
             jnp.zeros((LANES - GLA_CHUNK - SUBLANES, GLA_KEY_WIDTH), F32)], axis=0)
        xt = x.T
        kd_ts.append(xt[:, 0:GLA_CHUNK].astype(BF16))
        dec_cols.append(xt[:, GLA_CHUNK:GLA_CHUNK + 1])
    for b in range(BATCH):
        v_bb = gather(SLAB_V, GLA_WIDTH // LANES, b).astype(BF16)
        v_bbs.append(v_bb)
        v_stack = jnp.concatenate(
            [v_bb[:, hh * GLA_DV:(hh + 1) * GLA_DV] for hh in range(GLA_HEADS)],
            axis=0)
        s_b = gla_state_ref[b]
        o = _dot(ps[b], v_stack) + _dot(q_blks[b], s_b.astype(BF16))
        o = o * lax.rsqrt(jnp.mean(o * o, axis=-1, keepdims=True) + RMS_EPS) * norm_g
        for hh in range(GLA_HEADS):
            oslab_ref[hh, pl.ds(b, TT, stride=BATCH), :] = (
                o[hh * GLA_CHUNK:(hh + 1) * GLA_CHUNK, :])
    for b in range(BATCH):
        upd = jnp.concatenate(
            [_dot(kd_ts[b][hh * GLA_DK:(hh + 1) * GLA_DK, :],
                  v_bbs[b][:, hh * GLA_DV:(hh + 1) * GLA_DV])
             for hh in range(GLA_HEADS)], axis=0)
        gla_state_ref[b] = dec_cols[b] * gla_state_ref[b] + upd


def _mlp_kernel(last, h_ref, w1_ref, w2_ref, g_ref, b_ref, out_ref, yslab_ref=None):
    ys = []
    for rb in range(MLP_ROWS // MLP_ROW_BLOCK):
        h = h_ref[rb * MLP_ROW_BLOCK:(rb + 1) * MLP_ROW_BLOCK, :]
        hb = h.astype(BF16)
        hid = []
        for c in range(D_FF // MLP_FF_CHUNK):
            cs = slice(c * MLP_FF_CHUNK, (c + 1) * MLP_FF_CHUNK)
            a = jnp.maximum(_dot(hb, w1_ref[:, cs]), 0.0)
            hid.append((a * a).astype(BF16))
        ff = _dot(jnp.concatenate(hid, axis=1), w2_ref[...])
        ys.append(_layer_norm(DEEPNORM_ALPHA * h + ff, g_ref[...], b_ref[...]))
    y = jnp.concatenate(ys, axis=0)
    if last:
        for j in range(D_MODEL // LANES):
            yslab_ref[j] = y[:, j * LANES:(j + 1) * LANES]
        for b in range(BATCH):
            for j in range(D_MODEL // LANES):
                out_ref[b, :, j * LANES:(j + 1) * LANES] = (
                    yslab_ref[j, pl.ds(b, MLP_TT, stride=BATCH), :])
    else:
        out_ref[...] = y


def _const_spec(shape):
    nd = len(shape)
    return pl.BlockSpec(shape, lambda i, _nd=nd: (0,) * _nd)


def _mixer_call(first, h, weights):
    n_tiles = SEQ // TT
    tile_in = lambda i: jnp.minimum(i, n_tiles - 1)
    tile_out = lambda i: jnp.maximum(i - 1, 0)
    if first:
        in_specs = [pl.BlockSpec((BATCH, TT, D_MODEL), lambda i: (0, tile_in(i), 0))]
        extra_scratch = [pltpu.VMEM((D_MODEL // LANES, ROWS, LANES), F32)]
    else:
        in_specs = [pl.BlockSpec((TT, BATCH, D_MODEL), lambda i: (tile_in(i), 0, 0))]
        extra_scratch = []
    in_specs += [_const_spec(w.shape) for w in weights]
    return pl.pallas_call(
        functools.partial(_mixer_kernel, first),
        grid=(n_tiles + 1,),
        in_specs=in_specs,
        out_specs=pl.BlockSpec((TT, BATCH, D_MODEL), lambda i: (tile_out(i), 0, 0)),
        out_shape=jax.ShapeDtypeStruct((SEQ, BATCH, D_MODEL), F32),
        scratch_shapes=[
            pltpu.VMEM((BATCH, 2 * S5_NSTATE), F32),
            pltpu.VMEM((BATCH, GLA_HEADS * GLA_DK, GLA_DV), F32),
            pltpu.VMEM((BATCH, LRU_WIDTH), F32),
            pltpu.VMEM((HIST + ROWS, LRU_WIDTH), F32),
            pltpu.VMEM((S5_CROWS, 2 * S5_NSTATE), F32),
            pltpu.VMEM((S5_CROWS, 2 * S5_NSTATE), BF16),
            pltpu.VMEM((S5_NCHUNK, S5_CHUNK, BATCH, S5_WIDTH), F32),
            pltpu.VMEM((S5_NCHUNK, S5_CHUNK, BATCH, S5_WIDTH), F32),
            pltpu.VMEM((N_SLABS, ROWS, LANES), F32),
            pltpu.VMEM((GLA_WIDTH // LANES, ROWS, LANES), F32),
            pltpu.VMEM((ROWS, LRU_WIDTH), F32),
            pltpu.VMEM((ROWS, LRU_WIDTH), F32),
            pltpu.VMEM((ROWS, D_MODEL), F32),
            pltpu.VMEM((ROWS, GLA_WIDTH), F32),
        ] + extra_scratch,
        compiler_params=pltpu.CompilerParams(
            dimension_semantics=("arbitrary",), vmem_limit_bytes=VMEM_LIMIT),
        name="mixer",
    )(h, *weights)


def _mlp_call(last, h2d, w1, w2, g, b):
    n = h2d.shape[0]
    if last:
        out_specs = pl.BlockSpec((BATCH, MLP_TT, D_MODEL), lambda i: (0, i, 0))
        out_shape = jax.ShapeDtypeStruct((BATCH, SEQ, D_MODEL), F32)
        scratch = [pltpu.VMEM((D_MODEL // LANES, MLP_ROWS, LANES), F32)]
    else:
        out_specs = pl.BlockSpec((MLP_ROWS, D_MODEL), lambda i: (i, 0))
        out_shape = jax.ShapeDtypeStruct((n, D_MODEL), F32)
        scratch = []
    return pl.pallas_call(
        functools.partial(_mlp_kernel, last),
        grid=(n // MLP_ROWS,),
        in_specs=[pl.BlockSpec((MLP_ROWS, D_MODEL), lambda i: (i, 0)),
                  _const_spec(w1.shape), _const_spec(w2.shape),
                  _const_spec(g.shape), _const_spec(b.shape)],
        out_specs=out_specs,
        out_shape=out_shape,
        scratch_shapes=scratch,
        compiler_params=pltpu.CompilerParams(
            dimension_semantics=("arbitrary",), vmem_limit_bytes=VMEM_LIMIT),
        name="mlp",
    )(h2d, w1, w2, g, b)


def _block_diag(w):
    n, a, b = w.shape
    eye = jnp.eye(n, dtype=w.dtype)
    return jnp.einsum('gab,gh->gahb', w, eye).reshape(n * a, n * b)


def _cmul(ar, ai, br, bi):
    return ar * br - ai * bi, ar * bi + ai * br


def _s5_params(lam_re, lam_im, log_dt, b_re, b_im, c_re, c_im):
    dt = jnp.exp(log_dt)[:, None]
    lr = jnp.minimum(lam_re, -S5_MIN_DECAY)
    li = lam_im
    mag = jnp.exp(lr * dt)
    a_re = mag * jnp.cos(li * dt)
    a_im = mag * jnp.sin(li * dt)
    den = lr * lr + li * li
    f_re = ((a_re - 1.0) * lr + a_im * li) / den
    f_im = (a_im * lr - (a_re - 1.0) * li) / den
    bb_re = f_re[..., None] * b_re - f_im[..., None] * b_im
    bb_im = f_re[..., None] * b_im + f_im[..., None] * b_re
    a2_re, a2_im = _cmul(a_re, a_im, a_re, a_im)
    ab_re, ab_im = _cmul(a_re[..., None], a_im[..., None], bb_re, bb_im)
    ca1_re, ca1_im = _cmul(c_re, c_im, a_re[:, None, :], a_im[:, None, :])
    ca2_re, ca2_im = _cmul(c_re, c_im, a2_re[:, None, :], a2_im[:, None, :])
    ein = functools.partial(jnp.einsum, 'gcp,gpd->gcd', precision=lax.Precision.HIGHEST)
    k0 = ein(c_re, bb_re) - ein(c_im, bb_im)
    k1 = ein(ca1_re, bb_re) - ein(ca1_im, bb_im)
    to_state = lambda m: _block_diag(m.transpose(0, 2, 1))
    from_state = lambda m: _block_diag(m.transpose(0, 2, 1))
    in_out = lambda m: _block_diag(m.transpose(0, 2, 1))
    m_b = (to_state(bb_re), to_state(bb_im))
    m_ab = (to_state(ab_re), to_state(ab_im))
    m_ca1 = (from_state(ca1_re), from_state(ca1_im))
    m_ca2 = (from_state(ca2_re), from_state(ca2_im))
    m_k0, m_k1 = in_out(k0), in_out(k1)
    a2, w_e, w_y, w_t = [], [], [], []
    for n in range(S5_PARTS):
        li = slice(n * LANES, (n + 1) * LANES)
        si = slice(n * S5_PART_STATES, (n + 1) * S5_PART_STATES)
        a2 += [a2_re.reshape(1, -1)[:, si], a2_im.reshape(1, -1)[:, si]]
        w_e.append(jnp.concatenate([
            jnp.concatenate([m_ab[0][li, si], m_ab[1][li, si]], axis=1),
            jnp.concatenate([m_b[0][li, si], m_b[1][li, si]], axis=1)], axis=0))
        w_y.append(jnp.concatenate([
            jnp.concatenate([m_ca1[0][si, li], m_ca2[0][si, li]], axis=1),
            jnp.concatenate([-m_ca1[1][si, li], -m_ca2[1][si, li]], axis=1)], axis=0))
        w_t.append(jnp.concatenate([
            jnp.concatenate([m_k0[li, li], m_k1[li, li]], axis=1),
            jnp.concatenate([jnp.zeros((LANES, LANES), F32), m_k0[li, li]], axis=1)], axis=0))
    return (jnp.concatenate(a2, axis=1), jnp.stack(w_e).astype(BF16),
            jnp.stack(w_y).astype(BF16), jnp.stack(w_t).astype(BF16))


def kernel(x, ln_in_g, ln_in_b, w_in, s5_lambda_re, s5_lambda_im, s5_log_dt, s5_b_re, s5_b_im, s5_c_re, s5_c_im, s5_d, s5_w_glu, gla_w_gate_up, gla_b_gate, gla_norm_g, lru_conv_w, lru_conv_b, lru_w_r, lru_b_r, lru_w_i, lru_b_i, lru_lambda, w_out, ln1_g, ln1_b, mlp_w1, mlp_w2, ln2_g, ln2_b):
    assert x.shape == (BATCH, SEQ, D_MODEL)
    row = lambda t: t.reshape(1, -1).astype(F32)
    gz0 = OFF_R + GLA_WIDTH
    h = x
    for l in range(DEPTH):
        w_main = jnp.concatenate(
            [w_in[l][:, :gz0], w_in[l][:, gz0 + GLA_GATE_RANK:]], axis=1).astype(BF16)
        w_gz = jnp.pad(w_in[l][:, gz0:gz0 + GLA_GATE_RANK],
                       ((0, 0), (0, GZ_PAD - GLA_GATE_RANK))).astype(BF16)
        w_up = jnp.pad(gla_w_gate_up[l], ((0, GZ_PAD - GLA_GATE_RANK), (0, 0))).astype(BF16)
        s5_a2, w_e, w_y, w_t = _s5_params(s5_lambda_re[l], s5_lambda_im[l], s5_log_dt[l],
                                          s5_b_re[l], s5_b_im[l], s5_c_re[l], s5_c_im[l])
        weights = [
            row(ln_in_g), row(ln_in_b), w_main, w_gz,
            s5_a2, w_e, w_y, w_t, row(s5_d[l]), s5_w_glu[l].astype(BF16),
            w_up, row(gla_b_gate[l]), row(gla_norm_g[l]),
            lru_conv_w[l].astype(F32), row(lru_conv_b[l]),
            _block_diag(lru_w_r[l]).astype(BF16), row(lru_b_r[l]),
            _block_diag(lru_w_i[l]).astype(BF16), row(lru_b_i[l]),
            row(jax.nn.softplus(-lru_lambda[l])),
            w_out[l].astype(BF16), row(ln1_g[l]), row(ln1_b[l]),
        ]
        h = _mixer_call(l == 0, h, weights)
        last = l == DEPTH - 1
        h = _mlp_call(last, h.reshape(SEQ * BATCH, D_MODEL), mlp_w1[l].astype(BF16),
                      mlp_w2[l].astype(BF16), row(ln2_g[l]), row(ln2_b[l]))
        if not last:
            h = h.reshape(SEQ, BATCH, D_MODEL)
    return h
```

```python
import functools
import math

import jax
import jax.numpy as jnp
from jax import lax
from jax.experimental import pallas as pl
from jax.experimental.pallas import tpu as pltpu

F32 = jnp.float32
BF16 = jnp.bfloat16

D_MODEL = 1024
BATCH = 8
SEQ = 8192
DEPTH = 2
S5_WIDTH = 256
S5_GROUP = 16
S5_GROUPS = 16
S5_STATE = 64
S5_MIN_DECAY = 1e-4
S5_NSTATE = S5_GROUPS * S5_STATE
GLA_WIDTH = 512
GLA_HEADS = 4
GLA_DV = 128
GLA_DK = 64
GLA_KEY_WIDTH = 256
GLA_GATE_RANK = 16
GLA_TAU = 16.0
GLA_CHUNK = 64
LRU_WIDTH = 256
LRU_BLOCKS = 8
LRU_BLOCK = 32
LRU_CONV = 4
LRU_C = 8.0
D_FF = 4 * D_MODEL
DEEPNORM_ALPHA = (2 * DEPTH) ** 0.25
LN_EPS = 1e-5
RMS_EPS = 1e-6

LANES = 128
SUBLANES = 8
TT = GLA_CHUNK
ROWS = TT * BATCH
HIST = (LRU_CONV - 1) * BATCH
GZ_PAD = LANES
OFF_S5 = 0
OFF_Q = OFF_S5 + S5_WIDTH
OFF_K = OFF_Q + GLA_KEY_WIDTH
OFF_V = OFF_K + GLA_KEY_WIDTH
OFF_R = OFF_V + GLA_WIDTH
OFF_LX = OFF_R + GLA_WIDTH
OFF_LG = OFF_LX + LRU_WIDTH
D_IN_MAIN = OFF_LG + LRU_WIDTH
SLAB_QE = 0
SLAB_KE = SLAB_QE + GLA_KEY_WIDTH // LANES
SLAB_KD = SLAB_KE + GLA_KEY_WIDTH // LANES
SLAB_V = SLAB_KD + GLA_KEY_WIDTH // LANES
N_SLABS = SLAB_V + GLA_WIDTH // LANES
S5_CHUNK = 2
S5_PARTS = S5_WIDTH // LANES
S5_PART_STATES = S5_NSTATE // S5_PARTS
S5_NCHUNK = TT // S5_CHUNK
S5_CROWS = S5_NCHUNK * BATCH
MLP_FF_CHUNK = 1024
VMEM_LIMIT = 62 * 1024 * 1024


def _layer_norm(x, g, b):
    mu = jnp.mean(x, axis=-1, keepdims=True)
    xc = x - mu
    var = jnp.mean(xc * xc, axis=-1, keepdims=True)
    return xc * lax.rsqrt(var + LN_EPS) * g + b


_dot = functools.partial(jnp.dot, preferred_element_type=F32)


def _log_sigmoid(x):
    return jnp.minimum(x, 0.0) - jnp.log(1.0 + jnp.exp(-jnp.abs(x)))


def _shift_rows(x, n):
    return jnp.concatenate([jnp.zeros((n, x.shape[1]), x.dtype), x[:-n]], axis=0)


def _layer_kernel(first, last,
                  h_ref, lng_ref, lnb_ref, w_in_ref, w_gz_ref,
                  s5_a2_ref, w_e_ref, w_y_ref, w_t_ref, s5_d_ref, w_glu_ref,
                  w_up_ref, b_gate_ref, norm_g_ref,
                  conv_w_ref, conv_b_ref, w_r_ref, b_r_ref, w_i_ref, b_i_ref, sp_ref,
                  w_out_ref, ln1g_ref, ln1b_ref, w1_ref, w2_ref, ln2g_ref, ln2b_ref,
                  out_ref,
                  s5_state_ref, gla_state_ref, lru_state_ref, xs_ref,
                  e_ref, hs_ref, u4_ref, y4_ref, slab_ref, oslab_ref, la_ref, lb_ref,
                  resid_ref, silu_ref, *reorder_refs):
    step = pl.program_id(0)
    reorder_refs = list(reorder_refs)
    xslab_ref = reorder_refs.pop(0) if first else None
    yslab_ref = reorder_refs.pop(0) if last else None

    @pl.when(step == 0)
    def _():
        s5_state_ref[...] = jnp.zeros_like(s5_state_ref)
        gla_state_ref[...] = jnp.zeros_like(gla_state_ref)
        lru_state_ref[...] = jnp.zeros_like(lru_state_ref)
        xs_ref[0:HIST, :] = jnp.zeros((HIST, LRU_WIDTH), F32)
        oslab_ref[...] = jnp.zeros_like(oslab_ref)
        resid_ref[...] = jnp.zeros_like(resid_ref)
        silu_ref[...] = jnp.zeros_like(silu_ref)

    h1_halves = []
    for hf in range(2):
        rs = slice(hf * (ROWS // 2), (hf + 1) * (ROWS // 2))
        o_hf = jnp.concatenate(
            [oslab_ref[j, rs, :] for j in range(GLA_WIDTH // LANES)], axis=1)
        y_gla = o_hf * silu_ref[rs, :]
        pre = resid_ref[rs, :] + _dot(y_gla.astype(BF16),
                                      w_out_ref[S5_WIDTH:S5_WIDTH + GLA_WIDTH, :])
        h1_halves.append(_layer_norm(pre, ln1g_ref[...], ln1b_ref[...]))
    h1 = jnp.concatenate(h1_halves, axis=0)
    h1b = h1.astype(BF16)

    mlp = {}

    n_pieces = 2 * (D_FF // MLP_FF_CHUNK)

    def mlp_piece(kk):
        c = kk // 2
        cs = slice(c * MLP_FF_CHUNK, (c + 1) * MLP_FF_CHUNK)
        if kk % 2 == 0:
            a = jnp.maximum(_dot(h1b, w1_ref[:, cs]), 0.0)
            mlp['a'] = (a * a).astype(BF16)
        elif kk < n_pieces - 1:
            part = _dot(mlp['a'], w2_ref[cs, :])
            mlp['ff'] = part if c == 0 else mlp['ff'] + part
        else:
            for hf in range(2):
                rs = slice(hf * (ROWS // 2), (hf + 1) * (ROWS // 2))
                ff = mlp['ff'][rs, :] + _dot(mlp['a'][rs, :], w2_ref[cs, :])
                y2 = _layer_norm(DEEPNORM_ALPHA * h1[rs, :] + ff, ln2g_ref[...], ln2b_ref[...])
                if last:
                    for j in range(D_MODEL // LANES):
                        yslab_ref[j, rs, :] = y2[:, j * LANES:(j + 1) * LANES]
                else:
                    out_ref[hf * (TT // 2):(hf + 1) * (TT // 2)] = y2.reshape(
                        TT // 2, BATCH, D_MODEL)
            if last:
                for b in range(BATCH):
                    for j in range(D_MODEL // LANES):
                        out_ref[b, :, j * LANES:(j + 1) * LANES] = (
                            yslab_ref[j, pl.ds(b, TT, stride=BATCH), :])

    if first:
        for b in range(BATCH):
            for j in range(D_MODEL // LANES):
                xslab_ref[j, pl.ds(b, TT, stride=BATCH), :] = (
                    h_ref[b, :, j * LANES:(j + 1) * LANES])
        h = jnp.concatenate([xslab_ref[j] for j in range(D_MODEL // LANES)], axis=1)
        h = _layer_norm(h, lng_ref[...], lnb_ref[...])
    else:
        h = h_ref[...].reshape(ROWS, D_MODEL)
    hb = h.astype(BF16)

    u = _dot(hb, w_in_ref[:, OFF_S5:OFF_S5 + S5_WIDTH])
    q = _dot(hb, w_in_ref[:, OFF_Q:OFF_Q + GLA_KEY_WIDTH]) * (GLA_DK ** -0.5)
    k = _dot(hb, w_in_ref[:, OFF_K:OFF_K + GLA_KEY_WIDTH])
    u4_ref[...] = u.reshape(S5_NCHUNK, S5_CHUNK, BATCH, S5_WIDTH)
    uc = []
    for n in range(S5_PARTS):
        ls = slice(n * LANES, (n + 1) * LANES)
        uc.append(jnp.concatenate(
            [u4_ref[:, s, :, ls].reshape(S5_CROWS, LANES) for s in range(S5_CHUNK)],
            axis=1).astype(BF16))
        e_ref[:, n * 2 * S5_PART_STATES:(n + 1) * 2 * S5_PART_STATES] = _dot(uc[n], w_e_ref[n])
    v = _dot(hb, w_in_ref[:, OFF_V:OFF_V + GLA_WIDTH])
    gz = _dot(hb, w_gz_ref[...])
    xl = _dot(hb, w_in_ref[:, OFF_LX:OFF_LX + LRU_WIDTH])

    mlp_piece(0)
    mlp_piece(1)

    for n in range(S5_PARTS):
        lo = n * 2 * S5_PART_STATES
        re_sl = slice(lo, lo + S5_PART_STATES)
        im_sl = slice(lo + S5_PART_STATES, lo + 2 * S5_PART_STATES)
        a_re = jnp.broadcast_to(s5_a2_ref[0:1, re_sl], (BATCH, S5_PART_STATES))
        a_im = jnp.broadcast_to(s5_a2_ref[0:1, im_sl], (BATCH, S5_PART_STATES))
        h_re = s5_state_ref[:, re_sl]
        h_im = s5_state_ref[:, im_sl]
        for j in range(0, S5_NCHUNK, 2):
            pair_re, pair_im = [], []
            for jj in (j, j + 1):
                rows = slice(jj * BATCH, (jj + 1) * BATCH)
                pair_re.append(h_re)
                pair_im.append(h_im)
                n_re = a_re * h_re - a_im * h_im + e_ref[rows, re_sl]
                n_im = a_re * h_im + a_im * h_re + e_ref[rows, im_sl]
                h_re, h_im = n_re, n_im
            rows2 = slice(j * BATCH, (j + 2) * BATCH)
            hs_ref[rows2, re_sl] = jnp.concatenate(pair_re, axis=0).astype(BF16)
            hs_ref[rows2, im_sl] = jnp.concatenate(pair_im, axis=0).astype(BF16)
        s5_state_ref[:, re_sl] = h_re
        s5_state_ref[:, im_sl] = h_im

    g_pre = _dot(gz.astype(BF16), w_up_ref[...]) + b_gate_ref[...]
    gb = _dot(hb, w_in_ref[:, OFF_LG:OFF_LG + LRU_WIDTH])
    r = _dot(hb, w_in_ref[:, OFF_R:OFF_R + GLA_WIDTH])
    silu_ref[...] = r * jax.nn.sigmoid(r)

    mlp_piece(2)

    xs_ref[HIST:HIST + ROWS, :] = xl
    xc = conv_b_ref[...] + conv_w_ref[LRU_CONV - 1:LRU_CONV, :] * xl
    for kk in range(LRU_CONV - 1):
        xc = xc + conv_w_ref[kk:kk + 1, :] * xs_ref[kk * BATCH:kk * BATCH + ROWS, :]
    xs_ref[0:HIST, :] = xl[ROWS - HIST:ROWS, :]
    xcb = xc.astype(BF16)
    gate_r = jax.nn.sigmoid(_dot(xcb, w_r_ref[...]) + b_r_ref[...])
    gate_i = jax.nn.sigmoid(_dot(xcb, w_i_ref[...]) + b_i_ref[...])
    log_a = -LRU_C * gate_r * sp_ref[...]
    a_lru = jnp.exp(log_a)
    la_ref[...] = a_lru
    lb_ref[...] = jnp.sqrt(1.0 - a_lru * a_lru) * (gate_i * xc)

    mlp_piece(3)

    g = _log_sigmoid(g_pre) * (1.0 / GLA_TAU)
    bcum = g
    sh = 1
    while sh < TT:
        bcum = bcum + _shift_rows(bcum, sh * BATCH)
        sh *= 2
    blast = bcum[ROWS - BATCH:ROWS, :]
    blast_rows = jnp.broadcast_to(blast[None], (TT, BATCH, GLA_KEY_WIDTH)).reshape(
        ROWS, GLA_KEY_WIDTH)
    qe = q * jnp.exp(bcum)
    ke = k * jnp.exp(-bcum)
    kd = k * jnp.exp(blast_rows - bcum)
    dec = jnp.exp(blast)
    for j in range(GLA_KEY_WIDTH // LANES):
        ls = slice(j * LANES, (j + 1) * LANES)
        slab_ref[SLAB_QE + j] = qe[:, ls]
        slab_ref[SLAB_KE + j] = ke[:, ls]
        slab_ref[SLAB_KD + j] = kd[:, ls]
    for j in range(GLA_WIDTH // LANES):
        slab_ref[SLAB_V + j] = v[:, j * LANES:(j + 1) * LANES]

    for n in range(S5_PARTS):
        lo = n * 2 * S5_PART_STATES
        yc = (_dot(hs_ref[:, lo:lo + 2 * S5_PART_STATES], w_y_ref[n])
              + _dot(uc[n], w_t_ref[n]))
        for s in range(S5_CHUNK):
            y4_ref[:, s, :, n * LANES:(n + 1) * LANES] = (
                yc[:, s * LANES:(s + 1) * LANES].reshape(S5_NCHUNK, BATCH, LANES))
    y = y4_ref[...].reshape(ROWS, S5_WIDTH) + s5_d_ref[...] * u
    y = jax.nn.gelu(y)
    y_s5 = y * jax.nn.sigmoid(_dot(y.astype(BF16), w_glu_ref[...]))
    mix = _dot(y_s5.astype(BF16), w_out_ref[0:S5_WIDTH, :])

    mlp_piece(4)

    hl = lru_state_ref[...]
    for t in range(TT):
        rows = slice(t * BATCH, (t + 1) * BATCH)
        hl = la_ref[rows, :] * hl + lb_ref[rows, :]
        lb_ref[rows, :] = hl
    lru_state_ref[...] = hl
    y_lru = lb_ref[...] * jax.nn.gelu(gb)
    mix = mix + _dot(y_lru.astype(BF16), w_out_ref[S5_WIDTH + GLA_WIDTH:, :])
    resid_ref[...] = DEEPNORM_ALPHA * h + mix

    mlp_piece(5)

    hc = GLA_HEADS * GLA_CHUNK
    row = lax.broadcasted_iota(jnp.int32, (hc, hc), 0)
    col = lax.broadcasted_iota(jnp.int32, (hc, hc), 1)
    head_mask = (row // GLA_CHUNK) == (col // GLA_DK)
    causal_mask = ((row // GLA_CHUNK) == (col // GLA_CHUNK)) & (
        (row % GLA_CHUNK) >= (col % GLA_CHUNK))
    norm_g = norm_g_ref[...]

    def gather(slab0, n, b):
        return jnp.concatenate(
            [slab_ref[slab0 + j, pl.ds(b, TT, stride=BATCH), :] for j in range(n)], axis=1)

    q_blks, ps, v_bbs, kd_ts, dec_cols = [], [], [], [], []
    for b in range(BATCH):
        qe_b = gather(SLAB_QE, GLA_KEY_WIDTH // LANES, b)
        ke_b = gather(SLAB_KE, GLA_KEY_WIDTH // LANES, b)
        q_blk = jnp.where(head_mask, jnp.concatenate([qe_b] * GLA_HEADS, axis=0),
                          0.0).astype(BF16)
        k_blk = jnp.where(head_mask, jnp.concatenate([ke_b] * GLA_HEADS, axis=0),
                          0.0).astype(BF16)
        scores = lax.dot_general(q_blk, k_blk, (((1,), (1,)), ((), ())),
                                 preferred_element_type=F32)
        q_blks.append(q_blk)
        ps.append(jnp.where(causal_mask, scores, 0.0).astype(BF16))
    for b in range(BATCH):
        kd_b = gather(SLAB_KD, GLA_KEY_WIDTH // LANES, b)
        x = jnp.concatenate(
            [kd_b, jnp.broadcast_to(dec[b:b + 1, :], (SUBLANES, GLA_KEY_WIDTH)),
             jnp.zeros((LANES - GLA_CHUNK - SUBLANES, GLA_KEY_WIDTH), F32)], axis=0)
        xt = x.T
        kd_ts.append(xt[:, 0:GLA_CHUNK].astype(BF16))
        dec_cols.append(xt[:, GLA_CHUNK:GLA_CHUNK + 1])
    mlp_piece(6)
    for b in range(BATCH):
        v_bb = gather(SLAB_V, GLA_WIDTH // LANES, b).astype(BF16)
        v_bbs.append(v_bb)
        v_stack = jnp.concatenate(
            [v_bb[:, hh * GLA_DV:(hh + 1) * GLA_DV] for hh in range(GLA_HEADS)],
            axis=0)
        s_b = gla_state_ref[b]
        o = _dot(ps[b], v_stack) + _dot(q_blks[b], s_b.astype(BF16))
        o = o * lax.rsqrt(jnp.mean(o * o, axis=-1, keepdims=True) + RMS_EPS) * norm_g
        for hh in range(GLA_HEADS):
            oslab_ref[hh, pl.ds(b, TT, stride=BATCH), :] = (
                o[hh * GLA_CHUNK:(hh + 1) * GLA_CHUNK, :])
    for b in range(BATCH):
        upd = jnp.concatenate(
            [_dot(kd_ts[b][hh * GLA_DK:(hh + 1) * GLA_DK, :],
                  v_bbs[b][:, hh * GLA_DV:(hh + 1) * GLA_DV])
             for hh in range(GLA_HEADS)], axis=0)
        gla_state_ref[b] = dec_cols[b] * gla_state_ref[b] + upd

    mlp_piece(7)


def _const_spec(shape):
    nd = len(shape)
    return pl.BlockSpec(shape, lambda i, _nd=nd: (0,) * _nd, pipeline_mode=pl.Buffered(1))


def _layer_call(first, last, h, weights):
    n_tiles = SEQ // TT
    tile_in = lambda i: jnp.minimum(i, n_tiles - 1)
    tile_out = lambda i: jnp.maximum(i - 1, 0)
    reorder_scratch = []
    if first:
        in_specs = [pl.BlockSpec((BATCH, TT, D_MODEL), lambda i: (0, tile_in(i), 0))]
        reorder_scratch.append(pltpu.VMEM((D_MODEL // LANES, ROWS, LANES), F32))
    else:
        in_specs = [pl.BlockSpec((TT, BATCH, D_MODEL), lambda i: (tile_in(i), 0, 0))]
    if last:
        out_specs = pl.BlockSpec((BATCH, TT, D_MODEL), lambda i: (0, tile_out(i), 0))
        out_shape = jax.ShapeDtypeStruct((BATCH, SEQ, D_MODEL), F32)
        reorder_scratch.append(pltpu.VMEM((D_MODEL // LANES, ROWS, LANES), F32))
    else:
        out_specs = pl.BlockSpec((TT, BATCH, D_MODEL), lambda i: (tile_out(i), 0, 0))
        out_shape = jax.ShapeDtypeStruct((SEQ, BATCH, D_MODEL), F32)
    in_specs += [_const_spec(w.shape) for w in weights]
    return pl.pallas_call(
        functools.partial(_layer_kernel, first, last),
        grid=(n_tiles + 1,),
        in_specs=in_specs,
        out_specs=out_specs,
        out_shape=out_shape,
        scratch_shapes=[
            pltpu.VMEM((BATCH, 2 * S5_NSTATE), F32),
            pltpu.VMEM((BATCH, GLA_HEADS * GLA_DK, GLA_DV), F32),
            pltpu.VMEM((BATCH, LRU_WIDTH), F32),
            pltpu.VMEM((HIST + ROWS, LRU_WIDTH), F32),
            pltpu.VMEM((S5_CROWS, 2 * S5_NSTATE), F32),
            pltpu.VMEM((S5_CROWS, 2 * S5_NSTATE), BF16),
            pltpu.VMEM((S5_NCHUNK, S5_CHUNK, BATCH, S5_WIDTH), F32),
            pltpu.VMEM((S5_NCHUNK, S5_CHUNK, BATCH, S5_WIDTH), F32),
            pltpu.VMEM((N_SLABS, ROWS, LANES), F32),
            pltpu.VMEM((GLA_WIDTH // LANES, ROWS, LANES), F32),
            pltpu.VMEM((ROWS, LRU_WIDTH), F32),
            pltpu.VMEM((ROWS, LRU_WIDTH), F32),
            pltpu.VMEM((ROWS, D_MODEL), F32),
            pltpu.VMEM((ROWS, GLA_WIDTH), F32),
        ] + reorder_scratch,
        compiler_params=pltpu.CompilerParams(
            dimension_semantics=("arbitrary",), vmem_limit_bytes=VMEM_LIMIT),
        name="layer",
    )(h, *weights)


def _block_diag(w):
    n, a, b = w.shape
    eye = jnp.eye(n, dtype=w.dtype)
    return jnp.einsum('gab,gh->gahb', w, eye).reshape(n * a, n * b)


def _cmul(ar, ai, br, bi):
    return ar * br - ai * bi, ar * bi + ai * br


def _s5_params(lam_re, lam_im, log_dt, b_re, b_im, c_re, c_im):
    dt = jnp.exp(log_dt)[:, None]
    lr = jnp.minimum(lam_re, -S5_MIN_DECAY)
    li = lam_im
    mag = jnp.exp(lr * dt)
    a_re = mag * jnp.cos(li * dt)
    a_im = mag * jnp.sin(li * dt)
    den = lr * lr + li * li
    f_re = ((a_re - 1.0) * lr + a_im * li) / den
    f_im = (a_im * lr - (a_re - 1.0) * li) / den
    bb_re = f_re[..., None] * b_re - f_im[..., None] * b_im
    bb_im = f_re[..., None] * b_im + f_im[..., None] * b_re
    a2_re, a2_im = _cmul(a_re, a_im, a_re, a_im)
    ab_re, ab_im = _cmul(a_re[..., None], a_im[..., None], bb_re, bb_im)
    ca1_re, ca1_im = _cmul(c_re, c_im, a_re[:, None, :], a_im[:, None, :])
    ca2_re, ca2_im = _cmul(c_re, c_im, a2_re[:, None, :], a2_im[:, None, :])
    ein = functools.partial(jnp.einsum, 'gcp,gpd->gcd', precision=lax.Precision.HIGHEST)
    k0 = ein(c_re, bb_re) - ein(c_im, bb_im)
    k1 = ein(ca1_re, bb_re) - ein(ca1_im, bb_im)
    to_state = lambda m: _block_diag(m.transpose(0, 2, 1))
    from_state = lambda m: _block_diag(m.transpose(0, 2, 1))
    in_out = lambda m: _block_diag(m.transpose(0, 2, 1))
    m_b = (to_state(bb_re), to_state(bb_im))
    m_ab = (to_state(ab_re), to_state(ab_im))
    m_ca1 = (from_state(ca1_re), from_state(ca1_im))
    m_ca2 = (from_state(ca2_re), from_state(ca2_im))
    m_k0, m_k1 = in_out(k0), in_out(k1)
    a2, w_e, w_y, w_t = [], [], [], []
    for n in range(S5_PARTS):
        li = slice(n * LANES, (n + 1) * LANES)
        si = slice(n * S5_PART_STATES, (n + 1) * S5_PART_STATES)
        a2 += [a2_re.reshape(1, -1)[:, si], a2_im.reshape(1, -1)[:, si]]
        w_e.append(jnp.concatenate([
            jnp.concatenate([m_ab[0][li, si], m_ab[1][li, si]], axis=1),
            jnp.concatenate([m_b[0][li, si], m_b[1][li, si]], axis=1)], axis=0))
        w_y.append(jnp.concatenate([
            jnp.concatenate([m_ca1[0][si, li], m_ca2[0][si, li]], axis=1),
            jnp.concatenate([-m_ca1[1][si, li], -m_ca2[1][si, li]], axis=1)], axis=0))
        w_t.append(jnp.concatenate([
            jnp.concatenate([m_k0[li, li], m_k1[li, li]], axis=1),
            jnp.concatenate([jnp.zeros((LANES, LANES), F32), m_k0[li, li]], axis=1)], axis=0))
    return (jnp.concatenate(a2, axis=1), jnp.stack(w_e).astype(BF16),
            jnp.stack(w_y).astype(BF16), jnp.stack(w_t).astype(BF16))


def kernel(x, ln_in_g, ln_in_b, w_in, s5_lambda_re, s5_lambda_im, s5_log_dt, s5_b_re, s5_b_im, s5_c_re, s5_c_im, s5_d, s5_w_glu, gla_w_gate_up, gla_b_gate, gla_norm_g, lru_conv_w, lru_conv_b, lru_w_r, lru_b_r, lru_w_i, lru_b_i, lru_lambda, w_out, ln1_g, ln1_b, mlp_w1, mlp_w2, ln2_g, ln2_b):
    assert x.shape == (BATCH, SEQ, D_MODEL)
    row = lambda t: t.reshape(1, -1).astype(F32)
    gz0 = OFF_R + GLA_WIDTH
    h = x
    for l in range(DEPTH):
        w_main = jnp.concatenate(
            [w_in[l][:, :gz0], w_in[l][:, gz0 + GLA_GATE_RANK:]], axis=1).astype(BF16)
        w_gz = jnp.pad(w_in[l][:, gz0:gz0 + GLA_GATE_RANK],
                       ((0, 0), (0, GZ_PAD - GLA_GATE_RANK))).astype(BF16)
        w_up = jnp.pad(gla_w_gate_up[l], ((0, GZ_PAD - GLA_GATE_RANK), (0, 0))).astype(BF16)
        s5_a2, w_e, w_y, w_t = _s5_params(s5_lambda_re[l], s5_lambda_im[l], s5_log_dt[l],
                                          s5_b_re[l], s5_b_im[l], s5_c_re[l], s5_c_im[l])
        weights = [
            row(ln_in_g), row(ln_in_b), w_main, w_gz,
            s5_a2, w_e, w_y, w_t, row(s5_d[l]), s5_w_glu[l].astype(BF16),
            w_up, row(gla_b_gate[l]), row(gla_norm_g[l]),
            lru_conv_w[l].astype(F32), row(lru_conv_b[l]),
            _block_diag(lru_w_r[l]).astype(BF16), row(lru_b_r[l]),
            _block_diag(lru_w_i[l]).astype(BF16), row(lru_b_i[l]),
            row(jax.nn.softplus(-lru_lambda[l])),
            w_out[l].astype(BF16), row(ln1_g[l]), row(ln1_b[l]),
            mlp_w1[l].astype(BF16), mlp_w2[l].astype(BF16), row(ln2_g[l]), row(ln2_b[l]),
        ]
        h = _layer_call(l == 0, l == DEPTH - 1, h, weights)
    return h
```

```python
import functools

import jax
import jax.numpy as jnp
from jax import lax
from jax.experimental import pallas as pl
from jax.experimental.pallas import tpu as pltpu

F32 = jnp.float32
BF16 = jnp.bfloat16

D_MODEL = 1024
BATCH = 8
SEQ = 8192
DEPTH = 2
S5_WIDTH = 256
S5_GROUP = 16
S5_GROUPS = 16
S5_STATE = 64
S5_MIN_DECAY = 1e-4
S5_NSTATE = S5_GROUPS * S5_STATE
GLA_WIDTH = 512
GLA_HEADS = 4
GLA_DV = 128
GLA_DK = 64
GLA_KEY_WIDTH = 256
GLA_GATE_RANK = 16
GLA_TAU = 16.0
GLA_CHUNK = 64
LRU_WIDTH = 256
LRU_BLOCKS = 8
LRU_BLOCK = 32
LRU_CONV = 4
LRU_C = 8.0
D_FF = 4 * D_MODEL
DEEPNORM_ALPHA = (2 * DEPTH) ** 0.25
LN_EPS = 1e-5
RMS_EPS = 1e-6

LANES = 128
SUBLANES = 8
TT = GLA_CHUNK
ROWS = TT * BATCH
HIST = (LRU_CONV - 1) * BATCH
GZ_PAD = LANES
OFF_S5 = 0
OFF_Q = OFF_S5 + S5_WIDTH
OFF_K = OFF_Q + GLA_KEY_WIDTH
OFF_V = OFF_K + GLA_KEY_WIDTH
OFF_R = OFF_V + GLA_WIDTH
OFF_LX = OFF_R + GLA_WIDTH
OFF_LG = OFF_LX + LRU_WIDTH
D_IN_MAIN = OFF_LG + LRU_WIDTH
SLAB_QE = 0
SLAB_KE = SLAB_QE + GLA_KEY_WIDTH // LANES
SLAB_KD = SLAB_KE + GLA_KEY_WIDTH // LANES
SLAB_V = SLAB_KD + GLA_KEY_WIDTH // LANES
N_SLABS = SLAB_V + GLA_WIDTH // LANES
S5_CHUNK = 2
S5_PARTS = S5_WIDTH // LANES
S5_PART_STATES = S5_NSTATE // S5_PARTS
S5_NCHUNK = TT // S5_CHUNK
S5_CROWS = S5_NCHUNK * BATCH
MLP_FF_CHUNK = 1024
VMEM_LIMIT = 62 * 1024 * 1024


def _layer_norm(x, g, b):
    mu = jnp.mean(x, axis=-1, keepdims=True)
    xc = x - mu
    var = jnp.mean(xc * xc, axis=-1, keepdims=True)
    return xc * lax.rsqrt(var + LN_EPS) * g + b


_dot = functools.partial(jnp.dot, preferred_element_type=F32)


def _log_sigmoid(x):
    return jnp.minimum(x, 0.0) - jnp.log(1.0 + jnp.exp(-jnp.abs(x)))


def _shift_rows(x, n):
    return jnp.concatenate([jnp.zeros((n, x.shape[1]), x.dtype), x[:-n]], axis=0)


def _layer_kernel(first, last,
                  h_ref, lng_ref, lnb_ref, w_in_ref, w_gz_ref,
                  s5_a2_ref, w_e_ref, w_y_ref, w_t_ref, s5_d_ref, w_glu_ref,
                  w_up_ref, b_gate_ref, norm_g_ref,
                  conv_w_ref, conv_b_ref, w_r_ref, b_r_ref, w_i_ref, b_i_ref, sp_ref,
                  w_out_ref, ln1g_ref, ln1b_ref, w1_ref, w2_ref, ln2g_ref, ln2b_ref,
                  out_ref,
                  s5_state_ref, gla_state_ref, lru_state_ref, xs_ref,
                  e_ref, hs_ref, u4_ref, y4_ref, slab_ref, oslab_ref, la_ref, lb_ref,
                  resid_ref, silu_ref, *reorder_refs):
    step = pl.program_id(0)
    reorder_refs = list(reorder_refs)
    xslab_ref = reorder_refs.pop(0) if first else None
    yslab_ref = reorder_refs.pop(0) if last else None

    @pl.when(step == 0)
    def _():
        s5_state_ref[...] = jnp.zeros_like(s5_state_ref)
        gla_state_ref[...] = jnp.zeros_like(gla_state_ref)
        lru_state_ref[...] = jnp.zeros_like(lru_state_ref)
        xs_ref[0:HIST, :] = jnp.zeros((HIST, LRU_WIDTH), F32)
        oslab_ref[...] = jnp.zeros_like(oslab_ref)
        resid_ref[...] = jnp.zeros_like(resid_ref)
        silu_ref[...] = jnp.zeros_like(silu_ref)

    h1_halves = []
    for hf in range(2):
        rs = slice(hf * (ROWS // 2), (hf + 1) * (ROWS // 2))
        o_hf = jnp.concatenate(
            [oslab_ref[j, rs, :] for j in range(GLA_WIDTH // LANES)], axis=1)
        y_gla = o_hf * silu_ref[rs, :]
        pre = resid_ref[rs, :] + _dot(y_gla.astype(BF16),
                                      w_out_ref[S5_WIDTH:S5_WIDTH + GLA_WIDTH, :])
        h1_halves.append(_layer_norm(pre, ln1g_ref[...], ln1b_ref[...]))
    h1 = jnp.concatenate(h1_halves, axis=0)
    h1b = h1.astype(BF16)

    mlp = {}

    n_pieces = 2 * (D_FF // MLP_FF_CHUNK)

    def mlp_piece(kk):
        c = kk // 2
        cs = slice(c * MLP_FF_CHUNK, (c + 1) * MLP_FF_CHUNK)
        if kk % 2 == 0:
            a = jnp.maximum(_dot(h1b, w1_ref[:, cs]), 0.0)
            mlp['a'] = (a * a).astype(BF16)
        elif kk < n_pieces - 1:
            part = _dot(mlp['a'], w2_ref[cs, :])
            mlp['ff'] = part if c == 0 else mlp['ff'] + part
        else:
            for hf in range(2):
                rs = slice(hf * (ROWS // 2), (hf + 1) * (ROWS // 2))
                ff = mlp['ff'][rs, :] + _dot(mlp['a'][rs, :], w2_ref[cs, :])
                y2 = _layer_norm(DEEPNORM_ALPHA * h1[rs, :] + ff, ln2g_ref[...], ln2b_ref[...])
                ts = slice(hf * (TT // 2), (hf + 1) * (TT // 2))
                if last:
                    for j in range(D_MODEL // LANES):
                        yslab_ref[j, rs, :] = y2[:, j * LANES:(j + 1) * LANES]
                    for b in range(BATCH):
                        for j in range(D_MODEL // LANES):
                            out_ref[b, ts, j * LANES:(j + 1) * LANES] = yslab_ref[
                                j, pl.ds(hf * (ROWS // 2) + b, TT // 2, stride=BATCH), :]
                else:
                    out_ref[ts] = y2.reshape(TT // 2, BATCH, D_MODEL)

    if first:
        for b in range(BATCH):
            for j in range(D_MODEL // LANES):
                xslab_ref[j, pl.ds(b, TT, stride=BATCH), :] = (
                    h_ref[b, :, j * LANES:(j + 1) * LANES])
        h = jnp.concatenate([xslab_ref[j] for j in range(D_MODEL // LANES)], axis=1)
        h = _layer_norm(h, lng_ref[...], lnb_ref[...])
    else:
        h = h_ref[...].reshape(ROWS, D_MODEL)
    hb = h.astype(BF16)

    u = _dot(hb, w_in_ref[:, OFF_S5:OFF_S5 + S5_WIDTH])
    q = _dot(hb, w_in_ref[:, OFF_Q:OFF_Q + GLA_KEY_WIDTH]) * (GLA_DK ** -0.5)
    k = _dot(hb, w_in_ref[:, OFF_K:OFF_K + GLA_KEY_WIDTH])
    u4_ref[...] = u.reshape(S5_NCHUNK, S5_CHUNK, BATCH, S5_WIDTH)
    uc = []
    for n in range(S5_PARTS):
        ls = slice(n * LANES, (n + 1) * LANES)
        uc.append(jnp.concatenate(
            [u4_ref[:, s, :, ls].reshape(S5_CROWS, LANES) for s in range(S5_CHUNK)],
            axis=1).astype(BF16))
        e_ref[:, n * 2 * S5_PART_STATES:(n + 1) * 2 * S5_PART_STATES] = _dot(uc[n], w_e_ref[n])
    v = _dot(hb, w_in_ref[:, OFF_V:OFF_V + GLA_WIDTH])
    gz = _dot(hb, w_gz_ref[...])
    xl = _dot(hb, w_in_ref[:, OFF_LX:OFF_LX + LRU_WIDTH])

    mlp_piece(0)
    mlp_piece(1)

    for n in range(S5_PARTS):
        lo = n * 2 * S5_PART_STATES
        re_sl = slice(lo, lo + S5_PART_STATES)
        im_sl = slice(lo + S5_PART_STATES, lo + 2 * S5_PART_STATES)
        a_re = jnp.broadcast_to(s5_a2_ref[0:1, re_sl], (BATCH, S5_PART_STATES))
        a_im = jnp.broadcast_to(s5_a2_ref[0:1, im_sl], (BATCH, S5_PART_STATES))
        h_re = s5_state_ref[:, re_sl]
        h_im = s5_state_ref[:, im_sl]
        for j in range(0, S5_NCHUNK, 2):
            pair_re, pair_im = [], []
            for jj in (j, j + 1):
                rows = slice(jj * BATCH, (jj + 1) * BATCH)
                pair_re.append(h_re)
                pair_im.append(h_im)
                n_re = a_re * h_re - a_im * h_im + e_ref[rows, re_sl]
                n_im = a_re * h_im + a_im * h_re + e_ref[rows, im_sl]
                h_re, h_im = n_re, n_im
            rows2 = slice(j * BATCH, (j + 2) * BATCH)
            hs_ref[rows2, re_sl] = jnp.concatenate(pair_re, axis=0).astype(BF16)
            hs_ref[rows2, im_sl] = jnp.concatenate(pair_im, axis=0).astype(BF16)
        s5_state_ref[:, re_sl] = h_re
        s5_state_ref[:, im_sl] = h_im

    g_pre = _dot(gz.astype(BF16), w_up_ref[...]) + b_gate_ref[...]
    gb = _dot(hb, w_in_ref[:, OFF_LG:OFF_LG + LRU_WIDTH])
    r = _dot(hb, w_in_ref[:, OFF_R:OFF_R + GLA_WIDTH])
    silu_ref[...] = r * jax.nn.sigmoid(r)

    mlp_piece(2)

    xs_ref[HIST:HIST + ROWS, :] = xl
    xc = conv_b_ref[...] + conv_w_ref[LRU_CONV - 1:LRU_CONV, :] * xl
    for kk in range(LRU_CONV - 1):
        xc = xc + conv_w_ref[kk:kk + 1, :] * xs_ref[kk * BATCH:kk * BATCH + ROWS, :]
    xs_ref[0:HIST, :] = xl[ROWS - HIST:ROWS, :]
    xcb = xc.astype(BF16)
    gate_r = jax.nn.sigmoid(_dot(xcb, w_r_ref[...]) + b_r_ref[...])
    gate_i = jax.nn.sigmoid(_dot(xcb, w_i_ref[...]) + b_i_ref[...])
    log_a = -LRU_C * gate_r * sp_ref[...]
    a_lru = jnp.exp(log_a)
    la_ref[...] = a_lru
    lb_ref[...] = jnp.sqrt(1.0 - a_lru * a_lru) * (gate_i * xc)

    mlp_piece(3)

    g = _log_sigmoid(g_pre) * (1.0 / GLA_TAU)
    bcum = g
    sh = 1
    while sh < TT:
        bcum = bcum + _shift_rows(bcum, sh * BATCH)
        sh *= 2
    blast = bcum[ROWS - BATCH:ROWS, :]
    blast_rows = jnp.broadcast_to(blast[None], (TT, BATCH, GLA_KEY_WIDTH)).reshape(
        ROWS, GLA_KEY_WIDTH)
    qe = q * jnp.exp(bcum)
    ke = k * jnp.exp(-bcum)
    kd = k * jnp.exp(blast_rows - bcum)
    dec = jnp.exp(blast)
    for j in range(GLA_KEY_WIDTH // LANES):
        ls = slice(j * LANES, (j + 1) * LANES)
        slab_ref[SLAB_QE + j] = qe[:, ls]
        slab_ref[SLAB_KE + j] = ke[:, ls]
        slab_ref[SLAB_KD + j] = kd[:, ls]
    for j in range(GLA_WIDTH // LANES):
        slab_ref[SLAB_V + j] = v[:, j * LANES:(j + 1) * LANES]

    for n in range(S5_PARTS):
        lo = n * 2 * S5_PART_STATES
        yc = (_dot(hs_ref[:, lo:lo + 2 * S5_PART_STATES], w_y_ref[n])
              + _dot(uc[n], w_t_ref[n]))
        for s in range(S5_CHUNK):
            y4_ref[:, s, :, n * LANES:(n + 1) * LANES] = (
                yc[:, s * LANES:(s + 1) * LANES].reshape(S5_NCHUNK, BATCH, LANES))
    y = y4_ref[...].reshape(ROWS, S5_WIDTH) + s5_d_ref[...] * u
    y = jax.nn.gelu(y)
    y_s5 = y * jax.nn.sigmoid(_dot(y.astype(BF16), w_glu_ref[...]))
    mix = _dot(y_s5.astype(BF16), w_out_ref[0:S5_WIDTH, :])

    mlp_piece(4)

    hl = lru_state_ref[...]
    for t in range(TT):
        rows = slice(t * BATCH, (t + 1) * BATCH)
        hl = la_ref[rows, :] * hl + lb_ref[rows, :]
        lb_ref[rows, :] = hl
    lru_state_ref[...] = hl
    y_lru = lb_ref[...] * jax.nn.gelu(gb)
    mix = mix + _dot(y_lru.astype(BF16), w_out_ref[S5_WIDTH + GLA_WIDTH:, :])
    resid_ref[...] = DEEPNORM_ALPHA * h + mix

    mlp_piece(5)

    hc = GLA_HEADS * GLA_CHUNK
    row = lax.broadcasted_iota(jnp.int32, (hc, hc), 0)
    col = lax.broadcasted_iota(jnp.int32, (hc, hc), 1)
    head_mask = (row // GLA_CHUNK) == (col // GLA_DK)
    causal_mask = ((row // GLA_CHUNK) == (col // GLA_CHUNK)) & (
        (row % GLA_CHUNK) >= (col % GLA_CHUNK))
    norm_g = norm_g_ref[...]

    def gather(slab0, n, b):
        return jnp.concatenate(
            [slab_ref[slab0 + j, pl.ds(b, TT, stride=BATCH), :] for j in range(n)], axis=1)

    q_blks, ps, v_bbs, kd_ts, dec_cols = [], [], [], [], []
    for b in range(BATCH):
        qe_b = gather(SLAB_QE, GLA_KEY_WIDTH // LANES, b)
        ke_b = gather(SLAB_KE, GLA_KEY_WIDTH // LANES, b)
        q_blk = jnp.where(head_mask, jnp.concatenate([qe_b] * GLA_HEADS, axis=0),
                          0.0).astype(BF16)
        k_blk = jnp.where(head_mask, jnp.concatenate([ke_b] * GLA_HEADS, axis=0),
                          0.0).astype(BF16)
        scores = lax.dot_general(q_blk, k_blk, (((1,), (1,)), ((), ())),
                                 preferred_element_type=F32)
        q_blks.append(q_blk)
        ps.append(jnp.where(causal_mask, scores, 0.0).astype(BF16))
    for b in range(BATCH):
        kd_b = gather(SLAB_KD, GLA_KEY_WIDTH // LANES, b)
        x = jnp.concatenate(
            [kd_b, jnp.broadcast_to(dec[b:b + 1, :], (SUBLANES, GLA_KEY_WIDTH)),
             jnp.zeros((LANES - GLA_CHUNK - SUBLANES, GLA_KEY_WIDTH), F32)], axis=0)
        xt = x.T
        kd_ts.append(xt[:, 0:GLA_CHUNK].astype(BF16))
        dec_cols.append(xt[:, GLA_CHUNK:GLA_CHUNK + 1])
    mlp_piece(6)
    for b in range(BATCH):
        v_bb = gather(SLAB_V, GLA_WIDTH // LANES, b).astype(BF16)
        v_bbs.append(v_bb)
        v_stack = jnp.concatenate(
            [v_bb[:, hh * GLA_DV:(hh + 1) * GLA_DV] for hh in range(GLA_HEADS)],
            axis=0)
        s_b = gla_state_ref[b]
        o = _dot(ps[b], v_stack) + _dot(q_blks[b], s_b.astype(BF16))
        o = o * lax.rsqrt(jnp.mean(o * o, axis=-1, keepdims=True) + RMS_EPS) * norm_g
        for hh in range(GLA_HEADS):
            oslab_ref[hh, pl.ds(b, TT, stride=BATCH), :] = (
                o[hh * GLA_CHUNK:(hh + 1) * GLA_CHUNK, :])
    for b in range(BATCH):
        upd = jnp.concatenate(
            [_dot(kd_ts[b][hh * GLA_DK:(hh + 1) * GLA_DK, :],
                  v_bbs[b][:, hh * GLA_DV:(hh + 1) * GLA_DV])
             for hh in range(GLA_HEADS)], axis=0)
        gla_state_ref[b] = dec_cols[b] * gla_state_ref[b] + upd

    mlp_piece(7)


def _layer_weight_spec(shape, layer):
    nd = len(shape)
    return pl.BlockSpec((None,) + tuple(shape[1:]),
                        lambda i, _nd=nd, _l=layer: (_l,) + (0,) * (_nd - 1),
                        pipeline_mode=pl.Buffered(1))


def _layer_call(first, last, h, weights):
    n_tiles = SEQ // TT
    tile_in = lambda i: jnp.minimum(i, n_tiles - 1)
    tile_out = lambda i: jnp.maximum(i - 1, 0)
    reorder_scratch = []
    if first:
        in_specs = [pl.BlockSpec((BATCH, TT, D_MODEL), lambda i: (0, tile_in(i), 0))]
        reorder_scratch.append(pltpu.VMEM((D_MODEL // LANES, ROWS, LANES), F32))
    else:
        in_specs = [pl.BlockSpec((TT, BATCH, D_MODEL), lambda i: (tile_in(i), 0, 0))]
    if last:
        out_specs = pl.BlockSpec((BATCH, TT, D_MODEL), lambda i: (0, tile_out(i), 0))
        out_shape = jax.ShapeDtypeStruct((BATCH, SEQ, D_MODEL), F32)
        reorder_scratch.append(pltpu.VMEM((D_MODEL // LANES, ROWS, LANES), F32))
    else:
        out_specs = pl.BlockSpec((TT, BATCH, D_MODEL), lambda i: (tile_out(i), 0, 0))
        out_shape = jax.ShapeDtypeStruct((SEQ, BATCH, D_MODEL), F32)
    in_specs += [_layer_weight_spec(w.shape, li) for w, li in weights]
    return pl.pallas_call(
        functools.partial(_layer_kernel, first, last),
        grid=(n_tiles + 1,),
        in_specs=in_specs,
        out_specs=out_specs,
        out_shape=out_shape,
        scratch_shapes=[
            pltpu.VMEM((BATCH, 2 * S5_NSTATE), F32),
            pltpu.VMEM((BATCH, GLA_HEADS * GLA_DK, GLA_DV), F32),
            pltpu.VMEM((BATCH, LRU_WIDTH), F32),
            pltpu.VMEM((HIST + ROWS, LRU_WIDTH), F32),
            pltpu.VMEM((S5_CROWS, 2 * S5_NSTATE), F32),
            pltpu.VMEM((S5_CROWS, 2 * S5_NSTATE), BF16),
            pltpu.VMEM((S5_NCHUNK, S5_CHUNK, BATCH, S5_WIDTH), F32),
            pltpu.VMEM((S5_NCHUNK, S5_CHUNK, BATCH, S5_WIDTH), F32),
            pltpu.VMEM((N_SLABS, ROWS, LANES), F32),
            pltpu.VMEM((GLA_WIDTH // LANES, ROWS, LANES), F32),
            pltpu.VMEM((ROWS, LRU_WIDTH), F32),
            pltpu.VMEM((ROWS, LRU_WIDTH), F32),
            pltpu.VMEM((ROWS, D_MODEL), F32),
            pltpu.VMEM((ROWS, GLA_WIDTH), F32),
        ] + reorder_scratch,
        compiler_params=pltpu.CompilerParams(
            dimension_semantics=("arbitrary",), vmem_limit_bytes=VMEM_LIMIT),
        name="layer",
    )(h, *[w for w, _ in weights])


def _block_diag(w):
    n, a, b = w.shape
    eye = jnp.eye(n, dtype=w.dtype)
    return jnp.einsum('gab,gh->gahb', w, eye).reshape(n * a, n * b)


def _cmul(ar, ai, br, bi):
    return ar * br - ai * bi, ar * bi + ai * br


def _s5_params(lam_re, lam_im, log_dt, b_re, b_im, c_re, c_im):
    dt = jnp.exp(log_dt)[:, None]
    lr = jnp.minimum(lam_re, -S5_MIN_DECAY)
    li = lam_im
    mag = jnp.exp(lr * dt)
    a_re = mag * jnp.cos(li * dt)
    a_im = mag * jnp.sin(li * dt)
    den = lr * lr + li * li
    f_re = ((a_re - 1.0) * lr + a_im * li) / den
    f_im = (a_im * lr - (a_re - 1.0) * li) / den
    bb_re = f_re[..., None] * b_re - f_im[..., None] * b_im
    bb_im = f_re[..., None] * b_im + f_im[..., None] * b_re
    a2_re, a2_im = _cmul(a_re, a_im, a_re, a_im)
    ab_re, ab_im = _cmul(a_re[..., None], a_im[..., None], bb_re, bb_im)
    ca1_re, ca1_im = _cmul(c_re, c_im, a_re[:, None, :], a_im[:, None, :])
    ca2_re, ca2_im = _cmul(c_re, c_im, a2_re[:, None, :], a2_im[:, None, :])
    ein = functools.partial(jnp.einsum, 'gcp,gpd->gcd', precision=lax.Precision.HIGHEST)
    k0 = ein(c_re, bb_re) - ein(c_im, bb_im)
    k1 = ein(ca1_re, bb_re) - ein(ca1_im, bb_im)
    to_state = lambda m: _block_diag(m.transpose(0, 2, 1))
    from_state = lambda m: _block_diag(m.transpose(0, 2, 1))
    in_out = lambda m: _block_diag(m.transpose(0, 2, 1))
    m_b = (to_state(bb_re), to_state(bb_im))
    m_ab = (to_state(ab_re), to_state(ab_im))
    m_ca1 = (from_state(ca1_re), from_state(ca1_im))
    m_ca2 = (from_state(ca2_re), from_state(ca2_im))
    m_k0, m_k1 = in_out(k0), in_out(k1)
    a2, w_e, w_y, w_t = [], [], [], []
    for n in range(S5_PARTS):
        li = slice(n * LANES, (n + 1) * LANES)
        si = slice(n * S5_PART_STATES, (n + 1) * S5_PART_STATES)
        a2 += [a2_re.reshape(1, -1)[:, si], a2_im.reshape(1, -1)[:, si]]
        w_e.append(jnp.concatenate([
            jnp.concatenate([m_ab[0][li, si], m_ab[1][li, si]], axis=1),
            jnp.concatenate([m_b[0][li, si], m_b[1][li, si]], axis=1)], axis=0))
        w_y.append(jnp.concatenate([
            jnp.concatenate([m_ca1[0][si, li], m_ca2[0][si, li]], axis=1),
            jnp.concatenate([-m_ca1[1][si, li], -m_ca2[1][si, li]], axis=1)], axis=0))
        w_t.append(jnp.concatenate([
            jnp.concatenate([m_k0[li, li], m_k1[li, li]], axis=1),
            jnp.concatenate([jnp.zeros((LANES, LANES), F32), m_k0[li, li]], axis=1)], axis=0))
    return (jnp.concatenate(a2, axis=1), jnp.stack(w_e).astype(BF16),
            jnp.stack(w_y).astype(BF16), jnp.stack(w_t).astype(BF16))


def kernel(x, ln_in_g, ln_in_b, w_in, s5_lambda_re, s5_lambda_im, s5_log_dt, s5_b_re, s5_b_im, s5_c_re, s5_c_im, s5_d, s5_w_glu, gla_w_gate_up, gla_b_gate, gla_norm_g, lru_conv_w, lru_conv_b, lru_w_r, lru_b_r, lru_w_i, lru_b_i, lru_lambda, w_out, ln1_g, ln1_b, mlp_w1, mlp_w2, ln2_g, ln2_b):
    assert x.shape == (BATCH, SEQ, D_MODEL)
    rows = lambda t: t.reshape(t.shape[0], 1, -1).astype(F32)
    gz0 = OFF_R + GLA_WIDTH
    w_up = jnp.pad(gla_w_gate_up, ((0, 0), (0, GZ_PAD - GLA_GATE_RANK), (0, 0))).astype(BF16)
    s5_a2, w_e, w_y, w_t = jax.vmap(_s5_params)(
        s5_lambda_re, s5_lambda_im, s5_log_dt, s5_b_re, s5_b_im, s5_c_re, s5_c_im)
    stacked = [
        s5_a2, w_e, w_y, w_t, rows(s5_d), s5_w_glu.astype(BF16),
        w_up, rows(gla_b_gate), rows(gla_norm_g),
        lru_conv_w.astype(F32), rows(lru_conv_b),
        jax.vmap(_block_diag)(lru_w_r).astype(BF16), rows(lru_b_r),
        jax.vmap(_block_diag)(lru_w_i).astype(BF16), rows(lru_b_i),
        rows(jax.nn.softplus(-lru_lambda)),
        w_out.astype(BF16), rows(ln1_g), rows(ln1_b),
        mlp_w1.astype(BF16), mlp_w2.astype(BF16), rows(ln2_g), rows(ln2_b),
    ]
    ln_in = [(rows(ln_in_g[None]), 0), (rows(ln_in_b[None]), 0)]
    h = x
    for l in range(DEPTH):
        w_main = jnp.concatenate(
            [w_in[l][:, :gz0], w_in[l][:, gz0 + GLA_GATE_RANK:]], axis=1).astype(BF16)
        w_gz = jnp.pad(w_in[l][:, gz0:gz0 + GLA_GATE_RANK],
                       ((0, 0), (0, GZ_PAD - GLA_GATE_RANK))).astype(BF16)
        weights = ln_in + [(w_main[None], 0), (w_gz[None], 0)] + [(w, l) for w in stacked]
        h = _layer_call(l == 0, l == DEPTH - 1, h, weights)
    return h
```

```python
import functools

import jax
import jax.numpy as jnp
from jax import lax
from jax.experimental import pallas as pl
from jax.experimental.pallas import tpu as pltpu

F32 = jnp.float32
BF16 = jnp.bfloat16

D_MODEL = 1024
BATCH = 8
SEQ = 8192
DEPTH = 2
S5_WIDTH = 256
S5_GROUP = 16
S5_GROUPS = 16
S5_STATE = 64
S5_MIN_DECAY = 1e-4
S5_NSTATE = S5_GROUPS * S5_STATE
GLA_WIDTH = 512
GLA_HEADS = 4
GLA_DV = 128
GLA_DK = 64
GLA_KEY_WIDTH = 256
GLA_GATE_RANK = 16
GLA_TAU = 16.0
GLA_CHUNK = 64
LRU_WIDTH = 256
LRU_BLOCKS = 8
LRU_BLOCK = 32
LRU_CONV = 4
LRU_C = 8.0
D_FF = 4 * D_MODEL
DEEPNORM_ALPHA = (2 * DEPTH) ** 0.25
LN_EPS = 1e-5
RMS_EPS = 1e-6

LANES = 128
SUBLANES = 8
TT = GLA_CHUNK
ROWS = TT * BATCH
HIST = (LRU_CONV - 1) * BATCH
GZ_PAD = LANES
OFF_S5 = 0
OFF_Q = OFF_S5 + S5_WIDTH
OFF_K = OFF_Q + GLA_KEY_WIDTH
OFF_V = OFF_K + GLA_KEY_WIDTH
OFF_R = OFF_V + GLA_WIDTH
OFF_LX = OFF_R + GLA_WIDTH
OFF_LG = OFF_LX + LRU_WIDTH
D_IN_MAIN = OFF_LG + LRU_WIDTH
SLAB_QE = 0
SLAB_KE = SLAB_QE + GLA_KEY_WIDTH // LANES
SLAB_KD = SLAB_KE + GLA_KEY_WIDTH // LANES
SLAB_V = SLAB_KD + GLA_KEY_WIDTH // LANES
N_SLABS = SLAB_V + GLA_WIDTH // LANES
S5_CHUNK = 2
S5_PARTS = S5_WIDTH // LANES
S5_PART_STATES = S5_NSTATE // S5_PARTS
S5_NCHUNK = TT // S5_CHUNK
S5_CROWS = S5_NCHUNK * BATCH
MLP_FF_CHUNK = 1024
VMEM_LIMIT = 62 * 1024 * 1024


def _layer_norm(x, g, b):
    mu = jnp.mean(x, axis=-1, keepdims=True)
    xc = x - mu
    var = jnp.mean(xc * xc, axis=-1, keepdims=True)
    return xc * lax.rsqrt(var + LN_EPS) * g + b


_dot = functools.partial(jnp.dot, preferred_element_type=F32)


def _log_sigmoid(x):
    return jnp.minimum(x, 0.0) - jnp.log(1.0 + jnp.exp(-jnp.abs(x)))


def _shift_rows(x, n):
    return jnp.concatenate([jnp.zeros((n, x.shape[1]), x.dtype), x[:-n]], axis=0)


def _layer_kernel(first, last,
                  h_ref, lng_ref, lnb_ref, w_in_ref, w_gz_ref,
                  s5_a2_ref, w_e_ref, w_y_ref, w_t_ref, s5_d_ref, w_glu_ref,
                  w_up_ref, b_gate_ref, norm_g_ref,
                  conv_w_ref, conv_b_ref, w_r_ref, b_r_ref, w_i_ref, b_i_ref, sp_ref,
                  w_out_ref, ln1g_ref, ln1b_ref, w1_ref, w2_ref, ln2g_ref, ln2b_ref,
                  out_ref,
                  s5_state_ref, gla_state_ref, lru_state_ref, xs_ref,
                  e_ref, hs_ref, u4_ref, y4_ref, slab_ref, oslab_ref, la_ref, lb_ref,
                  resid_ref, silu_ref, *reorder_refs):
    step = pl.program_id(0)
    reorder_refs = list(reorder_refs)
    xslab_ref = reorder_refs.pop(0) if first else None
    yslab_ref = reorder_refs.pop(0) if last else None

    @pl.when(step == 0)
    def _():
        s5_state_ref[...] = jnp.zeros_like(s5_state_ref)
        gla_state_ref[...] = jnp.zeros_like(gla_state_ref)
        lru_state_ref[...] = jnp.zeros_like(lru_state_ref)
        xs_ref[0:HIST, :] = jnp.zeros((HIST, LRU_WIDTH), F32)
        oslab_ref[...] = jnp.zeros_like(oslab_ref)
        resid_ref[...] = jnp.zeros_like(resid_ref)
        silu_ref[...] = jnp.zeros_like(silu_ref)

    h1_halves = []
    for hf in range(2):
        rs = slice(hf * (ROWS // 2), (hf + 1) * (ROWS // 2))
        o_hf = jnp.concatenate(
            [oslab_ref[j, rs, :] for j in range(GLA_WIDTH // LANES)], axis=1)
        y_gla = o_hf * silu_ref[rs, :]
        pre = resid_ref[rs, :] + _dot(y_gla.astype(BF16),
                                      w_out_ref[S5_WIDTH:S5_WIDTH + GLA_WIDTH, :])
        h1_halves.append(_layer_norm(pre, ln1g_ref[...], ln1b_ref[...]))
    h1 = jnp.concatenate(h1_halves, axis=0)
    h1b = h1.astype(BF16)

    mlp = {}

    n_pieces = 2 * (D_FF // MLP_FF_CHUNK)

    def mlp_piece(kk):
        c = kk // 2
        cs = slice(c * MLP_FF_CHUNK, (c + 1) * MLP_FF_CHUNK)
        if kk % 2 == 0:
            a = jnp.maximum(_dot(h1b, w1_ref[:, cs]), 0.0)
            mlp['a'] = (a * a).astype(BF16)
        elif kk < n_pieces - 1:
            part = _dot(mlp['a'], w2_ref[cs, :])
            mlp['ff'] = part if c == 0 else mlp['ff'] + part
        else:
            for hf in range(2):
                rs = slice(hf * (ROWS // 2), (hf + 1) * (ROWS // 2))
                ff = mlp['ff'][rs, :] + _dot(mlp['a'][rs, :], w2_ref[cs, :])
                y2 = _layer_norm(DEEPNORM_ALPHA * h1[rs, :] + ff, ln2g_ref[...], ln2b_ref[...])
                ts = slice(hf * (TT // 2), (hf + 1) * (TT // 2))
                if last:
                    for j in range(D_MODEL // LANES):
                        yslab_ref[j, rs, :] = y2[:, j * LANES:(j + 1) * LANES]
                    for b in range(BATCH):
                        for j in range(D_MODEL // LANES):
                            out_ref[b, ts, j * LANES:(j + 1) * LANES] = yslab_ref[
                                j, pl.ds(hf * (ROWS // 2) + b, TT // 2, stride=BATCH), :]
                else:
                    out_ref[ts] = y2.reshape(TT // 2, BATCH, D_MODEL)

    if first:
        for b in range(BATCH):
            for j in range(D_MODEL // LANES):
                xslab_ref[j, pl.ds(b, TT, stride=BATCH), :] = (
                    h_ref[b, :, j * LANES:(j + 1) * LANES])
        h = jnp.concatenate([xslab_ref[j] for j in range(D_MODEL // LANES)], axis=1)
        h = _layer_norm(h, lng_ref[...], lnb_ref[...])
    else:
        h = h_ref[...].reshape(ROWS, D_MODEL)
    hb = h.astype(BF16)

    u = _dot(hb, w_in_ref[:, OFF_S5:OFF_S5 + S5_WIDTH])
    q = _dot(hb, w_in_ref[:, OFF_Q:OFF_Q + GLA_KEY_WIDTH]) * (GLA_DK ** -0.5)
    k = _dot(hb, w_in_ref[:, OFF_K:OFF_K + GLA_KEY_WIDTH])
    u4_ref[...] = u.reshape(S5_NCHUNK, S5_CHUNK, BATCH, S5_WIDTH)
    uc = []
    for n in range(S5_PARTS):
        ls = slice(n * LANES, (n + 1) * LANES)
        uc.append(jnp.concatenate(
            [u4_ref[:, s, :, ls].reshape(S5_CROWS, LANES) for s in range(S5_CHUNK)],
            axis=1).astype(BF16))
        e_ref[:, n * 2 * S5_PART_STATES:(n + 1) * 2 * S5_PART_STATES] = _dot(uc[n], w_e_ref[n])
    v = _dot(hb, w_in_ref[:, OFF_V:OFF_V + GLA_WIDTH])
    gz = _dot(hb, w_gz_ref[...])
    xl = _dot(hb, w_in_ref[:, OFF_LX:OFF_LX + LRU_WIDTH])

    mlp_piece(0)
    mlp_piece(1)

    for n in range(S5_PARTS):
        lo = n * 2 * S5_PART_STATES
        re_sl = slice(lo, lo + S5_PART_STATES)
        im_sl = slice(lo + S5_PART_STATES, lo + 2 * S5_PART_STATES)
        a_re = jnp.broadcast_to(s5_a2_ref[0:1, re_sl], (BATCH, S5_PART_STATES))
        a_im = jnp.broadcast_to(s5_a2_ref[0:1, im_sl], (BATCH, S5_PART_STATES))
        h_re = s5_state_ref[:, re_sl]
        h_im = s5_state_ref[:, im_sl]
        for j in range(0, S5_NCHUNK, 2):
            pair_re, pair_im = [], []
            for jj in (j, j + 1):
                rows = slice(jj * BATCH, (jj + 1) * BATCH)
                pair_re.append(h_re)
                pair_im.append(h_im)
                n_re = a_re * h_re - a_im * h_im + e_ref[rows, re_sl]
                n_im = a_re * h_im + a_im * h_re + e_ref[rows, im_sl]
                h_re, h_im = n_re, n_im
            rows2 = slice(j * BATCH, (j + 2) * BATCH)
            hs_ref[rows2, re_sl] = jnp.concatenate(pair_re, axis=0).astype(BF16)
            hs_ref[rows2, im_sl] = jnp.concatenate(pair_im, axis=0).astype(BF16)
        s5_state_ref[:, re_sl] = h_re
        s5_state_ref[:, im_sl] = h_im

    g_pre = _dot(gz.astype(BF16), w_up_ref[...]) + b_gate_ref[...]
    gb = _dot(hb, w_in_ref[:, OFF_LG:OFF_LG + LRU_WIDTH])
    r = _dot(hb, w_in_ref[:, OFF_R:OFF_R + GLA_WIDTH])
    silu_ref[...] = r * jax.nn.sigmoid(r)

    mlp_piece(2)

    xs_ref[HIST:HIST + ROWS, :] = xl
    xc = conv_b_ref[...] + conv_w_ref[LRU_CONV - 1:LRU_CONV, :] * xl
    for kk in range(LRU_CONV - 1):
        xc = xc + conv_w_ref[kk:kk + 1, :] * xs_ref[kk * BATCH:kk * BATCH + ROWS, :]
    xs_ref[0:HIST, :] = xl[ROWS - HIST:ROWS, :]
    xcb = xc.astype(BF16)
    gate_r = jax.nn.sigmoid(_dot(xcb, w_r_ref[...]) + b_r_ref[...])
    gate_i = jax.nn.sigmoid(_dot(xcb, w_i_ref[...]) + b_i_ref[...])
    log_a = -LRU_C * gate_r * sp_ref[...]
    la_ref[...] = jnp.exp(log_a)
    th = jnp.tanh(log_a)
    lb_ref[...] = jnp.sqrt(-2.0 * th / (1.0 - th)) * (gate_i * xc)

    mlp_piece(3)

    g = _log_sigmoid(g_pre) * (1.0 / GLA_TAU)
    bcum = g
    sh = 1
    while sh < TT:
        bcum = bcum + _shift_rows(bcum, sh * BATCH)
        sh *= 2
    blast = bcum[ROWS - BATCH:ROWS, :]
    blast_rows = jnp.broadcast_to(blast[None], (TT, BATCH, GLA_KEY_WIDTH)).reshape(
        ROWS, GLA_KEY_WIDTH)
    qe = q * jnp.exp(bcum)
    ke = k * jnp.exp(-bcum)
    kd = k * jnp.exp(blast_rows - bcum)
    dec = jnp.exp(blast)
    for j in range(GLA_KEY_WIDTH // LANES):
        ls = slice(j * LANES, (j + 1) * LANES)
        slab_ref[SLAB_QE + j] = qe[:, ls]
        slab_ref[SLAB_KE + j] = ke[:, ls]
        slab_ref[SLAB_KD + j] = kd[:, ls]
    for j in range(GLA_WIDTH // LANES):
        slab_ref[SLAB_V + j] = v[:, j * LANES:(j + 1) * LANES]

    for n in range(S5_PARTS):
        lo = n * 2 * S5_PART_STATES
        yc = (_dot(hs_ref[:, lo:lo + 2 * S5_PART_STATES], w_y_ref[n])
              + _dot(uc[n], w_t_ref[n]))
        for s in range(S5_CHUNK):
            y4_ref[:, s, :, n * LANES:(n + 1) * LANES] = (
                yc[:, s * LANES:(s + 1) * LANES].reshape(S5_NCHUNK, BATCH, LANES))
    y = y4_ref[...].reshape(ROWS, S5_WIDTH) + s5_d_ref[...] * u
    y = jax.nn.gelu(y)
    y_s5 = y * jax.nn.sigmoid(_dot(y.astype(BF16), w_glu_ref[...]))
    mix = _dot(y_s5.astype(BF16), w_out_ref[0:S5_WIDTH, :])

    mlp_piece(4)

    hl = lru_state_ref[...]
    for t in range(TT):
        rows = slice(t * BATCH, (t + 1) * BATCH)
        hl = la_ref[rows, :] * hl + lb_ref[rows, :]
        lb_ref[rows, :] = hl
    lru_state_ref[...] = hl
    y_lru = lb_ref[...] * jax.nn.gelu(gb)
    mix = mix + _dot(y_lru.astype(BF16), w_out_ref[S5_WIDTH + GLA_WIDTH:, :])
    resid_ref[...] = DEEPNORM_ALPHA * h + mix

    mlp_piece(5)

    hc = GLA_HEADS * GLA_CHUNK
    row = lax.broadcasted_iota(jnp.int32, (hc, hc), 0)
    col = lax.broadcasted_iota(jnp.int32, (hc, hc), 1)
    head_mask = (row // GLA_CHUNK) == (col // GLA_DK)
    causal_mask = ((row // GLA_CHUNK) == (col // GLA_CHUNK)) & (
        (row % GLA_CHUNK) >= (col % GLA_CHUNK))
    norm_g = norm_g_ref[...]

    def gather(slab0, n, b):
        return jnp.concatenate(
            [slab_ref[slab0 + j, pl.ds(b, TT, stride=BATCH), :] for j in range(n)], axis=1)

    q_blks, ps, v_bbs, kd_ts, dec_cols = [], [], [], [], []
    for b in range(BATCH):
        qe_b = gather(SLAB_QE, GLA_KEY_WIDTH // LANES, b)
        ke_b = gather(SLAB_KE, GLA_KEY_WIDTH // LANES, b)
        q_blk = jnp.where(head_mask, jnp.concatenate([qe_b] * GLA_HEADS, axis=0),
                          0.0).astype(BF16)
        k_blk = jnp.where(head_mask, jnp.concatenate([ke_b] * GLA_HEADS, axis=0),
                          0.0).astype(BF16)
        scores = lax.dot_general(q_blk, k_blk, (((1,), (1,)), ((), ())),
                                 preferred_element_type=F32)
        q_blks.append(q_blk)
        ps.append(jnp.where(causal_mask, scores, 0.0).astype(BF16))
    for b in range(BATCH):
        kd_b = gather(SLAB_KD, GLA_KEY_WIDTH // LANES, b)
        x = jnp.concatenate(
            [kd_b, jnp.broadcast_to(dec[b:b + 1, :], (SUBLANES, GLA_KEY_WIDTH)),
             jnp.zeros((LANES - GLA_CHUNK - SUBLANES, GLA_KEY_WIDTH), F32)], axis=0)
        xt = x.T
        kd_ts.append(xt[:, 0:GLA_CHUNK].astype(BF16))
        dec_cols.append(xt[:, GLA_CHUNK:GLA_CHUNK + 1])
    mlp_piece(6)
    for b in range(BATCH):
        v_bb = gather(SLAB_V, GLA_WIDTH // LANES, b).astype(BF16)
        v_bbs.append(v_bb)
        v_stack = jnp.concatenate(
            [v_bb[:, hh * GLA_DV:(hh + 1) * GLA_DV] for hh in range(GLA_HEADS)],
            axis=0)
        s_b = gla_state_ref[b]
        o = _dot(ps[b], v_stack) + _dot(q_blks[b], s_b.astype(BF16))
        o = o * lax.rsqrt(jnp.mean(o * o, axis=-1, keepdims=True) + RMS_EPS) * norm_g
        for hh in range(GLA_HEADS):
            oslab_ref[hh, pl.ds(b, TT, stride=BATCH), :] = (
                o[hh * GLA_CHUNK:(hh + 1) * GLA_CHUNK, :])
    for b in range(BATCH):
        upd = jnp.concatenate(
            [_dot(kd_ts[b][hh * GLA_DK:(hh + 1) * GLA_DK, :],
                  v_bbs[b][:, hh * GLA_DV:(hh + 1) * GLA_DV])
             for hh in range(GLA_HEADS)], axis=0)
        gla_state_ref[b] = dec_cols[b] * gla_state_ref[b] + upd

    mlp_piece(7)


def _layer_weight_spec(shape, layer):
    nd = len(shape)
    return pl.BlockSpec((None,) + tuple(shape[1:]),
                        lambda i, _nd=nd, _l=layer: (_l,) + (0,) * (_nd - 1),
                        pipeline_mode=pl.Buffered(1))


def _layer_call(first, last, h, weights):
    n_tiles = SEQ // TT
    tile_in = lambda i: jnp.minimum(i, n_tiles - 1)
    tile_out = lambda i: jnp.maximum(i - 1, 0)
    reorder_scratch = []
    if first:
        in_specs = [pl.BlockSpec((BATCH, TT, D_MODEL), lambda i: (0, tile_in(i), 0))]
        reorder_scratch.append(pltpu.VMEM((D_MODEL // LANES, ROWS, LANES), F32))
    else:
        in_specs = [pl.BlockSpec((TT, BATCH, D_MODEL), lambda i: (tile_in(i), 0, 0))]
    if last:
        out_specs = pl.BlockSpec((BATCH, TT, D_MODEL), lambda i: (0, tile_out(i), 0))
        out_shape = jax.ShapeDtypeStruct((BATCH, SEQ, D_MODEL), F32)
        reorder_scratch.append(pltpu.VMEM((D_MODEL // LANES, ROWS, LANES), F32))
    else:
        out_specs = pl.BlockSpec((TT, BATCH, D_MODEL), lambda i: (tile_out(i), 0, 0))
        out_shape = jax.ShapeDtypeStruct((SEQ, BATCH, D_MODEL), F32)
    in_specs += [_layer_weight_spec(w.shape, li) for w, li in weights]
    return pl.pallas_call(
        functools.partial(_layer_kernel, first, last),
        grid=(n_tiles + 1,),
        in_specs=in_specs,
        out_specs=out_specs,
        out_shape=out_shape,
        scratch_shapes=[
            pltpu.VMEM((BATCH, 2 * S5_NSTATE), F32),
            pltpu.VMEM((BATCH, GLA_HEADS * GLA_DK, GLA_DV), F32),
            pltpu.VMEM((BATCH, LRU_WIDTH), F32),
            pltpu.VMEM((HIST + ROWS, LRU_WIDTH), F32),
            pltpu.VMEM((S5_CROWS, 2 * S5_NSTATE), F32),
            pltpu.VMEM((S5_CROWS, 2 * S5_NSTATE), BF16),
            pltpu.VMEM((S5_NCHUNK, S5_CHUNK, BATCH, S5_WIDTH), F32),
            pltpu.VMEM((S5_NCHUNK, S5_CHUNK, BATCH, S5_WIDTH), F32),
            pltpu.VMEM((N_SLABS, ROWS, LANES), F32),
            pltpu.VMEM((GLA_WIDTH // LANES, ROWS, LANES), F32),
            pltpu.VMEM((ROWS, LRU_WIDTH), F32),
            pltpu.VMEM((ROWS, LRU_WIDTH), F32),
            pltpu.VMEM((ROWS, D_MODEL), F32),
            pltpu.VMEM((ROWS, GLA_WIDTH), F32),
        ] + reorder_scratch,
        compiler_params=pltpu.CompilerParams(
            dimension_semantics=("arbitrary",), vmem_limit_bytes=VMEM_LIMIT),
        name="layer",
    )(h, *[w for w, _ in weights])


def _block_diag(w):
    n, a, b = w.shape
    eye = jnp.eye(n, dtype=w.dtype)
    return jnp.einsum('gab,gh->gahb', w, eye).reshape(n * a, n * b)


def _cmul(ar, ai, br, bi):
    return ar * br - ai * bi, ar * bi + ai * br


def _s5_params(lam_re, lam_im, log_dt, b_re, b_im, c_re, c_im):
    dt = jnp.exp(log_dt)[:, None]
    lr = jnp.minimum(lam_re, -S5_MIN_DECAY)
    li = lam_im
    mag = jnp.exp(lr * dt)
    a_re = mag * jnp.cos(li * dt)
    a_im = mag * jnp.sin(li * dt)
    den = lr * lr + li * li
    f_re = ((a_re - 1.0) * lr + a_im * li) / den
    f_im = (a_im * lr - (a_re - 1.0) * li) / den
    bb_re = f_re[..., None] * b_re - f_im[..., None] * b_im
    bb_im = f_re[..., None] * b_im + f_im[..., None] * b_re
    a2_re, a2_im = _cmul(a_re, a_im, a_re, a_im)
    ab_re, ab_im = _cmul(a_re[..., None], a_im[..., None], bb_re, bb_im)
    ca1_re, ca1_im = _cmul(c_re, c_im, a_re[:, None, :], a_im[:, None, :])
    ca2_re, ca2_im = _cmul(c_re, c_im, a2_re[:, None, :], a2_im[:, None, :])
    ein = functools.partial(jnp.einsum, 'gcp,gpd->gcd', precision=lax.Precision.HIGHEST)
    k0 = ein(c_re, bb_re) - ein(c_im, bb_im)
    k1 = ein(ca1_re, bb_re) - ein(ca1_im, bb_im)
    to_state = lambda m: _block_diag(m.transpose(0, 2, 1))
    from_state = lambda m: _block_diag(m.transpose(0, 2, 1))
    in_out = lambda m: _block_diag(m.transpose(0, 2, 1))
    m_b = (to_state(bb_re), to_state(bb_im))
    m_ab = (to_state(ab_re), to_state(ab_im))
    m_ca1 = (from_state(ca1_re), from_state(ca1_im))
    m_ca2 = (from_state(ca2_re), from_state(ca2_im))
    m_k0, m_k1 = in_out(k0), in_out(k1)
    a2, w_e, w_y, w_t = [], [], [], []
    for n in range(S5_PARTS):
        li = slice(n * LANES, (n + 1) * LANES)
        si = slice(n * S5_PART_STATES, (n + 1) * S5_PART_STATES)
        a2 += [a2_re.reshape(1, -1)[:, si], a2_im.reshape(1, -1)[:, si]]
        w_e.append(jnp.concatenate([
            jnp.concatenate([m_ab[0][li, si], m_ab[1][li, si]], axis=1),
            jnp.concatenate([m_b[0][li, si], m_b[1][li, si]], axis=1)], axis=0))
        w_y.append(jnp.concatenate([
            jnp.concatenate([m_ca1[0][si, li], m_ca2[0][si, li]], axis=1),
            jnp.concatenate([-m_ca1[1][si, li], -m_ca2[1][si, li]], axis=1)], axis=0))
        w_t.append(jnp.concatenate([
            jnp.concatenate([m_k0[li, li], m_k1[li, li]], axis=1),
            jnp.concatenate([jnp.zeros((LANES, LANES), F32), m_k0[li, li]], axis=1)], axis=0))
    return (jnp.concatenate(a2, axis=1), jnp.stack(w_e).astype(BF16),
            jnp.stack(w_y).astype(BF16), jnp.stack(w_t).astype(BF16))


def kernel(x, ln_in_g, ln_in_b, w_in, s5_lambda_re, s5_lambda_im, s5_log_dt, s5_b_re, s5_b_im, s5_c_re, s5_c_im, s5_d, s5_w_glu, gla_w_gate_up, gla_b_gate, gla_norm_g, lru_conv_w, lru_conv_b, lru_w_r, lru_b_r, lru_w_i, lru_b_i, lru_lambda, w_out, ln1_g, ln1_b, mlp_w1, mlp_w2, ln2_g, ln2_b):
    assert x.shape == (BATCH, SEQ, D_MODEL)
    rows = lambda t: t.reshape(t.shape[0], 1, -1).astype(F32)
    gz0 = OFF_R + GLA_WIDTH
    w_up = jnp.pad(gla_w_gate_up, ((0, 0), (0, GZ_PAD - GLA_GATE_RANK), (0, 0))).astype(BF16)
    s5_a2, w_e, w_y, w_t = jax.vmap(_s5_params)(
        s5_lambda_re, s5_lambda_im, s5_log_dt, s5_b_re, s5_b_im, s5_c_re, s5_c_im)
    stacked = [
        s5_a2, w_e, w_y, w_t, rows(s5_d), s5_w_glu.astype(BF16),
        w_up, rows(gla_b_gate), rows(gla_norm_g),
        lru_conv_w.astype(F32), rows(lru_conv_b),
        jax.vmap(_block_diag)(lru_w_r).astype(BF16), rows(lru_b_r),
        jax.vmap(_block_diag)(lru_w_i).astype(BF16), rows(lru_b_i),
        rows(jax.nn.softplus(-lru_lambda)),
        w_out.astype(BF16), rows(ln1_g), rows(ln1_b),
        mlp_w1.astype(BF16), mlp_w2.astype(BF16), rows(ln2_g), rows(ln2_b),
    ]
    ln_in = [(rows(ln_in_g[None]), 0), (rows(ln_in_b[None]), 0)]
    h = x
    for l in range(DEPTH):
        w_main = jnp.concatenate(
            [w_in[l][:, :gz0], w_in[l][:, gz0 + GLA_GATE_RANK:]], axis=1).astype(BF16)
        w_gz = jnp.pad(w_in[l][:, gz0:gz0 + GLA_GATE_RANK],
                       ((0, 0), (0, GZ_PAD - GLA_GATE_RANK))).astype(BF16)
        weights = ln_in + [(w_main[None], 0), (w_gz[None], 0)] + [(w, l) for w in stacked]
        h = _layer_call(l == 0, l == DEPTH - 1, h, weights)
    return h
```

```python
import functools

import jax
import jax.numpy as jnp
from jax import lax
from jax.experimental import pallas as pl
from jax.experimental.pallas import tpu as pltpu

F32 = jnp.float32
BF16 = jnp.bfloat16

D_MODEL = 1024
BATCH = 8
SEQ = 8192
DEPTH = 2
S5_WIDTH = 256
S5_GROUP = 16
S5_GROUPS = 16
S5_STATE = 64
S5_MIN_DECAY = 1e-4
S5_NSTATE = S5_GROUPS * S5_STATE
GLA_WIDTH = 512
GLA_HEADS = 4
GLA_DV = 128
GLA_DK = 64
GLA_KEY_WIDTH = 256
GLA_GATE_RANK = 16
GLA_TAU = 16.0
GLA_CHUNK = 64
LRU_WIDTH = 256
LRU_BLOCKS = 8
LRU_BLOCK = 32
LRU_CONV = 4
LRU_C = 8.0
D_FF = 4 * D_MODEL
DEEPNORM_ALPHA = (2 * DEPTH) ** 0.25
LN_EPS = 1e-5
RMS_EPS = 1e-6

LANES = 128
SUBLANES = 8
TT = GLA_CHUNK
ROWS = TT * BATCH
HIST = (LRU_CONV - 1) * BATCH
GZ_PAD = LANES
OFF_S5 = 0
OFF_Q = OFF_S5 + S5_WIDTH
OFF_K = OFF_Q + GLA_KEY_WIDTH
OFF_V = OFF_K + GLA_KEY_WIDTH
OFF_R = OFF_V + GLA_WIDTH
OFF_LX = OFF_R + GLA_WIDTH
OFF_LG = OFF_LX + LRU_WIDTH
D_IN_MAIN = OFF_LG + LRU_WIDTH
SLAB_QE = 0
SLAB_KE = SLAB_QE + GLA_KEY_WIDTH // LANES
SLAB_V = SLAB_KE + GLA_KEY_WIDTH // LANES
N_SLABS = SLAB_V + GLA_WIDTH // LANES
S5_CHUNK = 2
S5_PARTS = S5_WIDTH // LANES
S5_PART_STATES = S5_NSTATE // S5_PARTS
S5_NCHUNK = TT // S5_CHUNK
S5_CROWS = S5_NCHUNK * BATCH
MLP_FF_CHUNK = 1024
VMEM_LIMIT = 62 * 1024 * 1024


def _layer_norm(x, g, b):
    mu = jnp.mean(x, axis=-1, keepdims=True)
    xc = x - mu
    var = jnp.mean(xc * xc, axis=-1, keepdims=True)
    return xc * lax.rsqrt(var + LN_EPS) * g + b


_dot = functools.partial(jnp.dot, preferred_element_type=F32)


def _log_sigmoid(x):
    return jnp.minimum(x, 0.0) - jnp.log(1.0 + jnp.exp(-jnp.abs(x)))


def _shift_rows(x, n):
    return jnp.concatenate([jnp.zeros((n, x.shape[1]), x.dtype), x[:-n]], axis=0)


def _layer_kernel(first, last,
                  h_ref, lng_ref, lnb_ref, w_in_ref, w_gz_ref,
                  s5_a2_ref, w_e_ref, w_y_ref, w_t_ref, s5_d_ref, w_glu_ref,
                  w_up_ref, b_gate_ref, norm_g_ref,
                  conv_w_ref, conv_b_ref, w_r_ref, b_r_ref, w_i_ref, b_i_ref, sp_ref,
                  w_out_ref, ln1g_ref, ln1b_ref, w1_ref, w2_ref, ln2g_ref, ln2b_ref,
                  out_ref,
                  s5_state_ref, gla_state_ref, lru_state_ref, xs_ref,
                  e_ref, hs_ref, u4_ref, y4_ref, slab_ref, oslab_ref, la_ref, lb_ref,
                  resid_ref, silu_ref, *reorder_refs):
    step = pl.program_id(0)
    reorder_refs = list(reorder_refs)
    xslab_ref = reorder_refs.pop(0) if first else None
    yslab_ref = reorder_refs.pop(0) if last else None

    @pl.when(step == 0)
    def _():
        s5_state_ref[...] = jnp.zeros_like(s5_state_ref)
        gla_state_ref[...] = jnp.zeros_like(gla_state_ref)
        lru_state_ref[...] = jnp.zeros_like(lru_state_ref)
        xs_ref[0:HIST, :] = jnp.zeros((HIST, LRU_WIDTH), F32)
        oslab_ref[...] = jnp.zeros_like(oslab_ref)
        resid_ref[...] = jnp.zeros_like(resid_ref)
        silu_ref[...] = jnp.zeros_like(silu_ref)

    h1_halves = []
    for hf in range(2):
        rs = slice(hf * (ROWS // 2), (hf + 1) * (ROWS // 2))
        o_hf = jnp.concatenate(
            [oslab_ref[j, rs, :] for j in range(GLA_WIDTH // LANES)], axis=1)
        y_gla = o_hf * silu_ref[rs, :]
        pre = resid_ref[rs, :] + _dot(y_gla.astype(BF16),
                                      w_out_ref[S5_WIDTH:S5_WIDTH + GLA_WIDTH, :])
        h1_halves.append(_layer_norm(pre, ln1g_ref[...], ln1b_ref[...]))
    h1 = jnp.concatenate(h1_halves, axis=0)
    h1b = h1.astype(BF16)

    mlp = {}

    n_pieces = 2 * (D_FF // MLP_FF_CHUNK)

    def mlp_piece(kk):
        c = kk // 2
        cs = slice(c * MLP_FF_CHUNK, (c + 1) * MLP_FF_CHUNK)
        if kk % 2 == 0:
            a = jnp.maximum(_dot(h1b, w1_ref[:, cs]), 0.0)
            mlp['a'] = (a * a).astype(BF16)
        elif kk < n_pieces - 1:
            part = _dot(mlp['a'], w2_ref[cs, :])
            mlp['ff'] = part if c == 0 else mlp['ff'] + part
        else:
            for hf in range(2):
                rs = slice(hf * (ROWS // 2), (hf + 1) * (ROWS // 2))
                ff = mlp['ff'][rs, :] + _dot(mlp['a'][rs, :], w2_ref[cs, :])
                y2 = _layer_norm(DEEPNORM_ALPHA * h1[rs, :] + ff, ln2g_ref[...], ln2b_ref[...])
                ts = slice(hf * (TT // 2), (hf + 1) * (TT // 2))
                if last:
                    for j in range(D_MODEL // LANES):
                        yslab_ref[j, rs, :] = y2[:, j * LANES:(j + 1) * LANES]
                    for b in range(BATCH):
                        for j in range(D_MODEL // LANES):
                            out_ref[b, ts, j * LANES:(j + 1) * LANES] = yslab_ref[
                                j, pl.ds(hf * (ROWS // 2) + b, TT // 2, stride=BATCH), :]
                else:
                    out_ref[ts] = y2.reshape(TT // 2, BATCH, D_MODEL)

    if first:
        for b in range(BATCH):
            for j in range(D_MODEL // LANES):
                xslab_ref[j, pl.ds(b, TT, stride=BATCH), :] = (
                    h_ref[b, :, j * LANES:(j + 1) * LANES])
        h = jnp.concatenate([xslab_ref[j] for j in range(D_MODEL // LANES)], axis=1)
        h = _layer_norm(h, lng_ref[...], lnb_ref[...])
    else:
        h = h_ref[...].reshape(ROWS, D_MODEL)
    hb = h.astype(BF16)

    u = _dot(hb, w_in_ref[:, OFF_S5:OFF_S5 + S5_WIDTH])
    q = _dot(hb, w_in_ref[:, OFF_Q:OFF_Q + GLA_KEY_WIDTH]) * (GLA_DK ** -0.5)
    k = _dot(hb, w_in_ref[:, OFF_K:OFF_K + GLA_KEY_WIDTH])
    u4_ref[...] = u.reshape(S5_NCHUNK, S5_CHUNK, BATCH, S5_WIDTH)
    uc = []
    for n in range(S5_PARTS):
        ls = slice(n * LANES, (n + 1) * LANES)
        uc.append(jnp.concatenate(
            [u4_ref[:, s, :, ls].reshape(S5_CROWS, LANES) for s in range(S5_CHUNK)],
            axis=1).astype(BF16))
        e_ref[:, n * 2 * S5_PART_STATES:(n + 1) * 2 * S5_PART_STATES] = _dot(uc[n], w_e_ref[n])
    v = _dot(hb, w_in_ref[:, OFF_V:OFF_V + GLA_WIDTH])
    gz = _dot(hb, w_gz_ref[...])
    xl = _dot(hb, w_in_ref[:, OFF_LX:OFF_LX + LRU_WIDTH])

    mlp_piece(0)
    mlp_piece(1)

    for n in range(S5_PARTS):
        lo = n * 2 * S5_PART_STATES
        re_sl = slice(lo, lo + S5_PART_STATES)
        im_sl = slice(lo + S5_PART_STATES, lo + 2 * S5_PART_STATES)
        a_re = jnp.broadcast_to(s5_a2_ref[0:1, re_sl], (BATCH, S5_PART_STATES))
        a_im = jnp.broadcast_to(s5_a2_ref[0:1, im_sl], (BATCH, S5_PART_STATES))
        h_re = s5_state_ref[:, re_sl]
        h_im = s5_state_ref[:, im_sl]
        for j in range(0, S5_NCHUNK, 2):
            pair_re, pair_im = [], []
            for jj in (j, j + 1):
                rows = slice(jj * BATCH, (jj + 1) * BATCH)
                pair_re.append(h_re)
                pair_im.append(h_im)
                n_re = a_re * h_re - a_im * h_im + e_ref[rows, re_sl]
                n_im = a_re * h_im + a_im * h_re + e_ref[rows, im_sl]
                h_re, h_im = n_re, n_im
            rows2 = slice(j * BATCH, (j + 2) * BATCH)
            hs_ref[rows2, re_sl] = jnp.concatenate(pair_re, axis=0).astype(BF16)
            hs_ref[rows2, im_sl] = jnp.concatenate(pair_im, axis=0).astype(BF16)
        s5_state_ref[:, re_sl] = h_re
        s5_state_ref[:, im_sl] = h_im

    g_pre = _dot(gz.astype(BF16), w_up_ref[...]) + b_gate_ref[...]
    gb = _dot(hb, w_in_ref[:, OFF_LG:OFF_LG + LRU_WIDTH])
    r = _dot(hb, w_in_ref[:, OFF_R:OFF_R + GLA_WIDTH])
    silu_ref[...] = r * jax.nn.sigmoid(r)

    mlp_piece(2)

    xs_ref[HIST:HIST + ROWS, :] = xl
    xc = conv_b_ref[...] + conv_w_ref[LRU_CONV - 1:LRU_CONV, :] * xl
    for kk in range(LRU_CONV - 1):
        xc = xc + conv_w_ref[kk:kk + 1, :] * xs_ref[kk * BATCH:kk * BATCH + ROWS, :]
    xs_ref[0:HIST, :] = xl[ROWS - HIST:ROWS, :]
    xcb = xc.astype(BF16)
    gate_r = jax.nn.sigmoid(_dot(xcb, w_r_ref[...]) + b_r_ref[...])
    gate_i = jax.nn.sigmoid(_dot(xcb, w_i_ref[...]) + b_i_ref[...])
    log_a = -LRU_C * gate_r * sp_ref[...]
    la_ref[...] = jnp.exp(log_a)
    th = jnp.tanh(log_a)
    lb_ref[...] = jnp.sqrt(-2.0 * th / (1.0 - th)) * (gate_i * xc)

    mlp_piece(3)

    g = _log_sigmoid(g_pre) * (1.0 / GLA_TAU)
    bcum = g
    sh = 1
    while sh < TT:
        bcum = bcum + _shift_rows(bcum, sh * BATCH)
        sh *= 2
    blast = bcum[ROWS - BATCH:ROWS, :]
    qe = q * jnp.exp(bcum)
    ke = k * jnp.exp(-bcum)
    dec = jnp.exp(blast)
    for j in range(GLA_KEY_WIDTH // LANES):
        ls = slice(j * LANES, (j + 1) * LANES)
        slab_ref[SLAB_QE + j] = qe[:, ls]
        slab_ref[SLAB_KE + j] = ke[:, ls]
    for j in range(GLA_WIDTH // LANES):
        slab_ref[SLAB_V + j] = v[:, j * LANES:(j + 1) * LANES]

    for n in range(S5_PARTS):
        lo = n * 2 * S5_PART_STATES
        yc = (_dot(hs_ref[:, lo:lo + 2 * S5_PART_STATES], w_y_ref[n])
              + _dot(uc[n], w_t_ref[n]))
        for s in range(S5_CHUNK):
            y4_ref[:, s, :, n * LANES:(n + 1) * LANES] = (
                yc[:, s * LANES:(s + 1) * LANES].reshape(S5_NCHUNK, BATCH, LANES))
    y = y4_ref[...].reshape(ROWS, S5_WIDTH) + s5_d_ref[...] * u
    y = jax.nn.gelu(y)
    y_s5 = y * jax.nn.sigmoid(_dot(y.astype(BF16), w_glu_ref[...]))
    mix = _dot(y_s5.astype(BF16), w_out_ref[0:S5_WIDTH, :])

    mlp_piece(4)

    hl = lru_state_ref[...]
    for t in range(TT):
        rows = slice(t * BATCH, (t + 1) * BATCH)
        hl = la_ref[rows, :] * hl + lb_ref[rows, :]
        lb_ref[rows, :] = hl
    lru_state_ref[...] = hl
    y_lru = lb_ref[...] * jax.nn.gelu(gb)
    mix = mix + _dot(y_lru.astype(BF16), w_out_ref[S5_WIDTH + GLA_WIDTH:, :])
    resid_ref[...] = DEEPNORM_ALPHA * h + mix

    mlp_piece(5)

    hc = GLA_HEADS * GLA_CHUNK
    row = lax.broadcasted_iota(jnp.int32, (hc, hc), 0)
    col = lax.broadcasted_iota(jnp.int32, (hc, hc), 1)
    head_mask = (row // GLA_CHUNK) == (col // GLA_DK)
    causal_mask = ((row // GLA_CHUNK) == (col // GLA_CHUNK)) & (
        (row % GLA_CHUNK) >= (col % GLA_CHUNK))
    norm_g = norm_g_ref[...]

    def gather(slab0, n, b):
        return jnp.concatenate(
            [slab_ref[slab0 + j, pl.ds(b, TT, stride=BATCH), :] for j in range(n)], axis=1)

    nt = (((1,), (1,)), ((), ()))
    q_blks, ps, kd_blks, v_ts = [], [], [], []
    for b in range(BATCH):
        qe_b = gather(SLAB_QE, GLA_KEY_WIDTH // LANES, b)
        ke_b = gather(SLAB_KE, GLA_KEY_WIDTH // LANES, b)
        q_blk = jnp.where(head_mask, jnp.concatenate([qe_b] * GLA_HEADS, axis=0),
                          0.0).astype(BF16)
        k_f32 = jnp.where(head_mask, jnp.concatenate([ke_b] * GLA_HEADS, axis=0), 0.0)
        k_blk = k_f32.astype(BF16)
        kd_blks.append((k_f32 * dec[b:b + 1, :]).astype(BF16))
        scores = lax.dot_general(q_blk, k_blk, nt, preferred_element_type=F32)
        q_blks.append(q_blk)
        ps.append(jnp.where(causal_mask, scores, 0.0).astype(BF16))
    for b in range(BATCH):
        v_b = gather(SLAB_V, GLA_WIDTH // LANES, b)
        v_ts.append(jnp.concatenate(
            [jnp.concatenate([v_b[:, (2 * p) * GLA_DV:(2 * p + 1) * GLA_DV],
                              v_b[:, (2 * p + 1) * GLA_DV:(2 * p + 2) * GLA_DV]], axis=0).T
             for p in range(GLA_HEADS // 2)], axis=1).astype(BF16))
    mlp_piece(6)
    for b in range(BATCH):
        s_t = gla_state_ref[b]
        o_t = (lax.dot_general(v_ts[b], ps[b], nt, preferred_element_type=F32)
               + lax.dot_general(s_t.astype(BF16), q_blks[b], nt,
                                 preferred_element_type=F32))
        o_t = o_t * lax.rsqrt(jnp.mean(o_t * o_t, axis=0, keepdims=True) + RMS_EPS) * norm_g
        for p in range(GLA_HEADS // 2):
            o_p = o_t[:, p * LANES:(p + 1) * LANES].T
            for hh in range(2):
                oslab_ref[2 * p + hh, pl.ds(b, TT, stride=BATCH), :] = (
                    o_p[hh * GLA_CHUNK:(hh + 1) * GLA_CHUNK, :])
    for b in range(BATCH):
        gla_state_ref[b] = (gla_state_ref[b] * dec[b:b + 1, :]
                            + _dot(v_ts[b], kd_blks[b]))

    mlp_piece(7)


def _layer_weight_spec(shape, layer):
    nd = len(shape)
    return pl.BlockSpec((None,) + tuple(shape[1:]),
                        lambda i, _nd=nd, _l=layer: (_l,) + (0,) * (_nd - 1),
                        pipeline_mode=pl.Buffered(1))


def _layer_call(first, last, h, weights):
    n_tiles = SEQ // TT
    tile_in = lambda i: jnp.minimum(i, n_tiles - 1)
    tile_out = lambda i: jnp.maximum(i - 1, 0)
    reorder_scratch = []
    if first:
        in_specs = [pl.BlockSpec((BATCH, TT, D_MODEL), lambda i: (0, tile_in(i), 0))]
        reorder_scratch.append(pltpu.VMEM((D_MODEL // LANES, ROWS, LANES), F32))
    else:
        in_specs = [pl.BlockSpec((TT, BATCH, D_MODEL), lambda i: (tile_in(i), 0, 0))]
    if last:
        out_specs = pl.BlockSpec((BATCH, TT, D_MODEL), lambda i: (0, tile_out(i), 0))
        out_shape = jax.ShapeDtypeStruct((BATCH, SEQ, D_MODEL), F32)
        reorder_scratch.append(pltpu.VMEM((D_MODEL // LANES, ROWS, LANES), F32))
    else:
        out_specs = pl.BlockSpec((TT, BATCH, D_MODEL), lambda i: (tile_out(i), 0, 0))
        out_shape = jax.ShapeDtypeStruct((SEQ, BATCH, D_MODEL), F32)
    in_specs += [_layer_weight_spec(w.shape, li) for w, li in weights]
    return pl.pallas_call(
        functools.partial(_layer_kernel, first, last),
        grid=(n_tiles + 1,),
        in_specs=in_specs,
        out_specs=out_specs,
        out_shape=out_shape,
        scratch_shapes=[
            pltpu.VMEM((BATCH, 2 * S5_NSTATE), F32),
            pltpu.VMEM((BATCH, GLA_DV, GLA_HEADS * GLA_DK), F32),
            pltpu.VMEM((BATCH, LRU_WIDTH), F32),
            pltpu.VMEM((HIST + ROWS, LRU_WIDTH), F32),
            pltpu.VMEM((S5_CROWS, 2 * S5_NSTATE), F32),
            pltpu.VMEM((S5_CROWS, 2 * S5_NSTATE), BF16),
            pltpu.VMEM((S5_NCHUNK, S5_CHUNK, BATCH, S5_WIDTH), F32),
            pltpu.VMEM((S5_NCHUNK, S5_CHUNK, BATCH, S5_WIDTH), F32),
            pltpu.VMEM((N_SLABS, ROWS, LANES), F32),
            pltpu.VMEM((GLA_WIDTH // LANES, ROWS, LANES), F32),
            pltpu.VMEM((ROWS, LRU_WIDTH), F32),
            pltpu.VMEM((ROWS, LRU_WIDTH), F32),
            pltpu.VMEM((ROWS, D_MODEL), F32),
            pltpu.VMEM((ROWS, GLA_WIDTH), F32),
        ] + reorder_scratch,
        compiler_params=pltpu.CompilerParams(
            dimension_semantics=("arbitrary",), vmem_limit_bytes=VMEM_LIMIT),
        name="layer",
    )(h, *[w for w, _ in weights])


def _block_diag(w):
    n, a, b = w.shape
    eye = jnp.eye(n, dtype=w.dtype)
    return jnp.einsum('gab,gh->gahb', w, eye).reshape(n * a, n * b)


def _cmul(ar, ai, br, bi):
    return ar * br - ai * bi, ar * bi + ai * br


def _s5_params(lam_re, lam_im, log_dt, b_re, b_im, c_re, c_im):
    dt = jnp.exp(log_dt)[:, None]
    lr = jnp.minimum(lam_re, -S5_MIN_DECAY)
    li = lam_im
    mag = jnp.exp(lr * dt)
    a_re = mag * jnp.cos(li * dt)
    a_im = mag * jnp.sin(li * dt)
    den = lr * lr + li * li
    f_re = ((a_re - 1.0) * lr + a_im * li) / den
    f_im = (a_im * lr - (a_re - 1.0) * li) / den
    bb_re = f_re[..., None] * b_re - f_im[..., None] * b_im
    bb_im = f_re[..., None] * b_im + f_im[..., None] * b_re
    a2_re, a2_im = _cmul(a_re, a_im, a_re, a_im)
    ab_re, ab_im = _cmul(a_re[..., None], a_im[..., None], bb_re, bb_im)
    ca1_re, ca1_im = _cmul(c_re, c_im, a_re[:, None, :], a_im[:, None, :])
    ca2_re, ca2_im = _cmul(c_re, c_im, a2_re[:, None, :], a2_im[:, None, :])
    ein = functools.partial(jnp.einsum, 'gcp,gpd->gcd', precision=lax.Precision.HIGHEST)
    k0 = ein(c_re, bb_re) - ein(c_im, bb_im)
    k1 = ein(ca1_re, bb_re) - ein(ca1_im, bb_im)
    to_state = lambda m: _block_diag(m.transpose(0, 2, 1))
    from_state = lambda m: _block_diag(m.transpose(0, 2, 1))
    in_out = lambda m: _block_diag(m.transpose(0, 2, 1))
    m_b = (to_state(bb_re), to_state(bb_im))
    m_ab = (to_state(ab_re), to_state(ab_im))
    m_ca1 = (from_state(ca1_re), from_state(ca1_im))
    m_ca2 = (from_state(ca2_re), from_state(ca2_im))
    m_k0, m_k1 = in_out(k0), in_out(k1)
    a2, w_e, w_y, w_t = [], [], [], []
    for n in range(S5_PARTS):
        li = slice(n * LANES, (n + 1) * LANES)
        si = slice(n * S5_PART_STATES, (n + 1) * S5_PART_STATES)
        a2 += [a2_re.reshape(1, -1)[:, si], a2_im.reshape(1, -1)[:, si]]
        w_e.append(jnp.concatenate([
            jnp.concatenate([m_ab[0][li, si], m_ab[1][li, si]], axis=1),
            jnp.concatenate([m_b[0][li, si], m_b[1][li, si]], axis=1)], axis=0))
        w_y.append(jnp.concatenate([
            jnp.concatenate([m_ca1[0][si, li], m_ca2[0][si, li]], axis=1),
            jnp.concatenate([-m_ca1[1][si, li], -m_ca2[1][si, li]], axis=1)], axis=0))
        w_t.append(jnp.concatenate([
            jnp.concatenate([m_k0[li, li], m_k1[li, li]], axis=1),
            jnp.concatenate([jnp.zeros((LANES, LANES), F32), m_k0[li, li]], axis=1)], axis=0))
    return (jnp.concatenate(a2, axis=1), jnp.stack(w_e).astype(BF16),
            jnp.stack(w_y).astype(BF16), jnp.stack(w_t).astype(BF16))


def kernel(x, ln_in_g, ln_in_b, w_in, s5_lambda_re, s5_lambda_im, s5_log_dt, s5_b_re, s5_b_im, s5_c_re, s5_c_im, s5_d, s5_w_glu, gla_w_gate_up, gla_b_gate, gla_norm_g, lru_conv_w, lru_conv_b, lru_w_r, lru_b_r, lru_w_i, lru_b_i, lru_lambda, w_out, ln1_g, ln1_b, mlp_w1, mlp_w2, ln2_g, ln2_b):
    assert x.shape == (BATCH, SEQ, D_MODEL)
    rows = lambda t: t.reshape(t.shape[0], 1, -1).astype(F32)
    gz0 = OFF_R + GLA_WIDTH
    w_up = jnp.pad(gla_w_gate_up, ((0, 0), (0, GZ_PAD - GLA_GATE_RANK), (0, 0))).astype(BF16)
    s5_a2, w_e, w_y, w_t = jax.vmap(_s5_params)(
        s5_lambda_re, s5_lambda_im, s5_log_dt, s5_b_re, s5_b_im, s5_c_re, s5_c_im)
    stacked = [
        s5_a2, w_e, w_y, w_t, rows(s5_d), s5_w_glu.astype(BF16),
        w_up, rows(gla_b_gate), gla_norm_g.reshape(DEPTH, GLA_DV, 1).astype(F32),
        lru_conv_w.astype(F32), rows(lru_conv_b),
        jax.vmap(_block_diag)(lru_w_r).astype(BF16), rows(lru_b_r),
        jax.vmap(_block_diag)(lru_w_i).astype(BF16), rows(lru_b_i),
        rows(jax.nn.softplus(-lru_lambda)),
        w_out.astype(BF16), rows(ln1_g), rows(ln1_b),
        mlp_w1.astype(BF16), mlp_w2.astype(BF16), rows(ln2_g), rows(ln2_b),
    ]
    ln_in = [(rows(ln_in_g[None]), 0), (rows(ln_in_b[None]), 0)]
    h = x
    for l in range(DEPTH):
        w_main = jnp.concatenate(
            [w_in[l][:, :gz0], w_in[l][:, gz0 + GLA_GATE_RANK:]], axis=1).astype(BF16)
        w_gz = jnp.pad(w_in[l][:, gz0:gz0 + GLA_GATE_RANK],
                       ((0, 0), (0, GZ_PAD - GLA_GATE_RANK))).astype(BF16)
        weights = ln_in + [(w_main[None], 0), (w_gz[None], 0)] + [(w, l) for w in stacked]
        h = _layer_call(l == 0, l == DEPTH - 1, h, weights)
    return h
```
